```python
import jax
import jax.numpy as jnp
from jax import lax
import numpy as np

D_MODEL = 2048
BATCH = 16
SEQ = 2048
DEPTH = 1
DEC_BATCH = 32
DEC_SEQ = 1
PAST_LEN = 16384
PAGE_SIZE = 128

HEAD_DIM = 128
NSA_HEADS = 8
NSA_KV = 2
NSA_GROUP = NSA_HEADS // NSA_KV
NSA_W = NSA_HEADS * HEAD_DIM
KV_W = NSA_KV * HEAD_DIM
CMP_LEN = 32
CMP_STRIDE = 16
CMP_HID = 256
SEL_LEN = 64
TOP_N = 16
WINDOW = 512
NSA_QBLK = 32
FORCE_SCORE = 1.0e4
SB_HEADS = 8
SB_W = SB_HEADS * HEAD_DIM
SB_QBLK = 128
PEER_HEADS = 8
PEER_TOPK = 16
N_KEYS = 128
N_EXPERTS = N_KEYS * N_KEYS
PEER_DK = 128
PEER_CHUNK = 128
NORM_EPS = 1e-6
PROJ_SIZES = (NSA_W, KV_W, KV_W, KV_W, KV_W, KV_W, KV_W, 3 * NSA_HEADS, SB_W, SB_W, SB_W, D_MODEL, D_MODEL)
PROJ_W = NSA_W + 6 * KV_W + 3 * NSA_HEADS + 3 * SB_W + 2 * D_MODEL

kernel_name = 'nsa_stickbreak_peer_hybrid_step'


def rms_norm(x, gain):
    xf = x.astype(jnp.float32)
    y = xf * lax.rsqrt(jnp.mean(xf * xf, axis=-1, keepdims=True) + NORM_EPS)
    return (y * gain.astype(jnp.float32)).astype(x.dtype)


def alibi_slopes():
    h = jnp.arange(1, NSA_HEADS + 1, dtype=jnp.float32)
    return jnp.exp2(-8.0 * h / NSA_HEADS).reshape(NSA_KV, NSA_GROUP)


def masked_softmax(s, mask, axis=-1):
    s = jnp.where(mask, s, -jnp.inf)
    m = jnp.max(s, axis=axis, keepdims=True)
    m = jnp.where(jnp.isfinite(m), m, 0.0)
    e = jnp.where(mask, jnp.exp(s - m), 0.0)
    den = jnp.sum(e, axis=axis, keepdims=True)
    return e / jnp.where(den > 0, den, 1.0)


def compress_rows(rows, pe, w1, w2):
    B, L, G, d = rows.shape
    n_chunk = L // CMP_STRIDE
    half = CMP_STRIDE * d
    ch = rows.reshape(B, n_chunk, CMP_STRIDE, G, d).transpose(0, 1, 3, 2, 4).reshape(B, n_chunk, G, half)
    pre = (jnp.einsum('bngf,fh->bngh', ch[:, :-1], w1[:half])
           + jnp.einsum('bngf,fh->bngh', ch[:, 1:], w1[half:])
           + pe.reshape(-1) @ w1)
    return jnp.einsum('bngh,hd->bngd', jax.nn.gelu(pre), w2)


def compressed_kv(rows, k_gain, pe, w1, w2):
    kc = rms_norm(compress_rows(rows[:, :, 0], pe[0], w1[0], w2[0]), k_gain)
    vc = compress_rows(rows[:, :, 1], pe[1], w1[1], w2[1])
    c_end = jnp.arange(kc.shape[1]) * CMP_STRIDE + (CMP_LEN - 1)
    return kc, vc, c_end


def selection_importance(p):
    ratio = SEL_LEN // CMP_STRIDE
    nsel = (p.shape[-1] + 1) // ratio
    pad = jnp.pad(p, [(0, 0)] * (p.ndim - 1) + [(1, 1)])
    first = pad[..., :ratio * nsel].reshape(p.shape[:-1] + (nsel, ratio)).sum(-1)
    return first + pad[..., ratio::ratio]


def nsa_attend(q, q_pos, gates, kc, vc, c_end, gather_sel, k_band, v_band, band_pos):
    scale = HEAD_DIM ** -0.5
    f32 = jnp.float32
    slopes = alibi_slopes()[None, None, :, :, None]
    tq = q_pos[:, None]
    s_c = jnp.einsum('btgjd,bngd->btgjn', q, kc, preferred_element_type=f32) * scale
    s_c = s_c - slopes * (tq - c_end[None, :]).astype(f32)[None, :, None, None, :]
    p_c = masked_softmax(s_c, (c_end[None, :] <= tq)[None, :, None, None, :])
    o_c = jnp.einsum('btgjn,bngd->btgjd', p_c.astype(vc.dtype), vc, preferred_element_type=f32)
    imp = selection_importance(p_c.sum(axis=3))
    nsel = imp.shape[-1]
    blk = jnp.arange(nsel)
    cur = (q_pos // SEL_LEN)[:, None]
    valid = (blk * SEL_LEN)[None, :] <= tq
    forced = (blk == 0)[None, :] | (blk[None, :] == cur) | (blk[None, :] == cur - 1)
    score = jnp.where(forced[None, :, None, :], FORCE_SCORE, jnp.where(valid[None, :, None, :], imp, -1.0))
    _, idx = lax.top_k(score, min(TOP_N, nsel))
    k_s, v_s = gather_sel(idx)
    dist_s = q_pos[None, :, None, None, None] - (idx[..., None] * SEL_LEN + jnp.arange(SEL_LEN))
    s_s = jnp.einsum('btgjd,btgnpd->btgjnp', q, k_s, preferred_element_type=f32) * scale
    s_s = s_s - slopes[..., None] * dist_s[:, :, :, None].astype(f32)
    p_s = masked_softmax(s_s, (dist_s >= 0)[:, :, :, None], axis=(-2, -1))
    o_s = jnp.einsum('btgjnp,btgnpd->btgjd', p_s.astype(v_s.dtype), v_s, preferred_element_type=f32)
    dist_w = tq - band_pos[None, :]
    mask_w = (dist_w >= 0) & (dist_w <= WINDOW) & (band_pos >= 0)[None, :]
    s_w = jnp.einsum('btgjd,bwgd->btgjw', q, k_band, preferred_element_type=f32) * scale
    s_w = s_w - slopes * dist_w.astype(f32)[None, :, None, None, :]
    p_w = masked_softmax(s_w, mask_w[None, :, None, None, :])
    o_w = jnp.einsum('btgjw,bwgd->btgjd', p_w.astype(v_band.dtype), v_band, preferred_element_type=f32)
    out = gates[..., 0:1] * o_c + gates[..., 1:2] * o_s + gates[..., 2:3] * o_w
    return out.astype(q.dtype)


def nsa_prompt(q, gates, cmp_kv, sel_kv, win_kv, k_gain_c, pe, w1, w2):
    B, T = q.shape[:2]
    kc, vc, c_end = compressed_kv(cmp_kv, k_gain_c, pe, w1, w2)
    sel_blocks = sel_kv.reshape(B, T // SEL_LEN, SEL_LEN, 2, NSA_KV, HEAD_DIM)
    bidx = jnp.arange(B)[:, None, None, None]
    gidx = jnp.arange(NSA_KV)[None, None, :, None]

    def gather_sel(idx):
        return sel_blocks[bidx, idx, :, 0, gidx], sel_blocks[bidx, idx, :, 1, gidx]

    band = jnp.pad(win_kv, ((0, 0), (WINDOW, 0), (0, 0), (0, 0), (0, 0)))

    def one_block(i):
        t0 = i * NSA_QBLK
        q_i = lax.dynamic_slice_in_dim(q, t0, NSA_QBLK, axis=1)
        g_i = lax.dynamic_slice_in_dim(gates, t0, NSA_QBLK, axis=1)
        bk = lax.dynamic_slice_in_dim(band, t0, WINDOW + NSA_QBLK, axis=1)
        band_pos = t0 - WINDOW + jnp.arange(WINDOW + NSA_QBLK)
        return nsa_attend(q_i, t0 + jnp.arange(NSA_QBLK), g_i, kc, vc, c_end, gather_sel,
                          bk[:, :, 0], bk[:, :, 1], band_pos)

    out = lax.map(one_block, jnp.arange(T // NSA_QBLK))
    return jnp.moveaxis(out, 0, 1).reshape(B, T, NSA_W)


def nsa_sample(q, gates, cmp_new, sel_new, win_new, cache_cmp_kv, cache_sel_kv, win_state, page_table, layer,
               k_gain_c, pe, w1, w2):
    B, T = q.shape[:2]
    past = page_table.shape[1] * PAGE_SIZE
    L = past + T
    L_pad = -(-L // SEL_LEN) * SEL_LEN
    cmp_past = cache_cmp_kv[layer, page_table].reshape(B, past, 2, NSA_KV, HEAD_DIM)
    cmp_all = jnp.pad(jnp.concatenate([cmp_past, cmp_new], axis=1),
                      ((0, 0), (0, L_pad - L), (0, 0), (0, 0), (0, 0)))
    kc, vc, c_end = compressed_kv(cmp_all, k_gain_c, pe, w1, w2)
    n_past_blk = past // SEL_LEN
    n_new_blk = L_pad // SEL_LEN - n_past_blk
    blk_per_page = PAGE_SIZE // SEL_LEN
    new_blocks = jnp.pad(sel_new, ((0, 0), (0, n_new_blk * SEL_LEN - T), (0, 0), (0, 0), (0, 0)))
    new_blocks = new_blocks.reshape(B, n_new_blk, SEL_LEN, 2, NSA_KV, HEAD_DIM)
    bidx = jnp.arange(B)[:, None, None, None]
    gidx = jnp.arange(NSA_KV)[None, None, :, None]

    def gather_sel(idx):
        jp = jnp.minimum(idx, n_past_blk - 1)
        page = page_table[bidx, jp // blk_per_page]
        row = (jp % blk_per_page)[..., None] * SEL_LEN + jnp.arange(SEL_LEN)
        jn = jnp.clip(idx - n_past_blk, 0, n_new_blk - 1)
        is_new = (idx >= n_past_blk)[..., None, None]
        k_past = cache_sel_kv[layer, page[..., None], row, 0, gidx[..., None]]
        v_past = cache_sel_kv[layer, page[..., None], row, 1, gidx[..., None]]
        k_sel = jnp.where(is_new, new_blocks[bidx, jn, :, 0, gidx], k_past)
        v_sel = jnp.where(is_new, new_blocks[bidx, jn, :, 1, gidx], v_past)
        return k_sel, v_sel

    band = jnp.concatenate([win_state, win_new], axis=1)
    wb = win_state.shape[1]
    band_pos = past - wb + jnp.arange(wb + T)
    out = nsa_attend(q, past + jnp.arange(T), gates, kc, vc, c_end, gather_sel,
                     band[:, :, 0], band[:, :, 1], band_pos)
    return out.reshape(B, T, NSA_W), band[:, -wb:]


def stick_break(z, mask, carry):
    log_keep = jnp.where(mask, jax.nn.log_sigmoid(-z), 0.0)
    later = lax.cumsum(log_keep, axis=log_keep.ndim - 1, reverse=True) - log_keep
    a = jnp.where(mask, jnp.exp(jax.nn.log_sigmoid(z) + later + carry[..., None]), 0.0)
    return a, carry + jnp.sum(log_keep, axis=-1)


def sb_prompt(q, kv):
    B, T = q.shape[:2]
    k, v = kv[:, :, 0], kv[:, :, 1]
    scale = HEAD_DIM ** -0.5
    key_pos = jnp.arange(T)

    def one_block(i):
        t0 = i * SB_QBLK
        q_i = lax.dynamic_slice_in_dim(q, t0, SB_QBLK, axis=1)
        z = jnp.einsum('bthd,bshd->bhts', q_i, k, preferred_element_type=jnp.float32) * scale
        mask = key_pos[None, :] < (t0 + jnp.arange(SB_QBLK))[:, None]
        a, _ = stick_break(z, mask, jnp.zeros(z.shape[:3], jnp.float32))
        return jnp.einsum('bhts,bshd->bthd', a.astype(v.dtype), v, preferred_element_type=jnp.float32)

    out = lax.map(one_block, jnp.arange(T // SB_QBLK))
    return jnp.moveaxis(out, 0, 1).reshape(B, T, SB_W).astype(q.dtype)


def sb_sample(q, kv_new, cache_sb_kv, page_table, layer):
    B, T = q.shape[:2]
    scale = HEAD_DIM ** -0.5
    t_pos = jnp.arange(T)
    z = jnp.einsum('bthd,bshd->bhts', q, kv_new[:, :, 0], preferred_element_type=jnp.float32) * scale
    a, carry = stick_break(z, t_pos[None, :] < t_pos[:, None], jnp.zeros((B, SB_HEADS, T), jnp.float32))
    o = jnp.einsum('bhts,bshd->bthd', a.astype(kv_new.dtype), kv_new[:, :, 1], preferred_element_type=jnp.float32)

    def page_step(state, phys):
        carry, o = state
        kv = cache_sb_kv[layer, phys]
        z = jnp.einsum('bthd,bshd->bhts', q, kv[:, :, 0], preferred_element_type=jnp.float32) * scale
        a, carry = stick_break(z, True, carry)
        o = o + jnp.einsum('bhts,bshd->bthd', a.astype(kv.dtype), kv[:, :, 1], preferred_element_type=jnp.float32)
        return (carry, o), None

    (carry, o), _ = lax.scan(page_step, (carry, o), page_table[:, ::-1].T)
    return o.reshape(B, T, SB_W).astype(q.dtype)


def peer_chunk(xc, w_q, sub_keys, u_tab, v_tab):
    C = xc.shape[0]
    q = (xc @ w_q).reshape(C, PEER_HEADS, 2, PEER_DK // 2)
    s = jnp.einsum('chpk,pnk->chpn', q, sub_keys, preferred_element_type=jnp.float32)
    s_half, i_half = lax.top_k(s, PEER_TOPK)
    cand = s_half[:, :, 0, :, None] + s_half[:, :, 1, None, :]
    best, flat = lax.top_k(cand.reshape(C, PEER_HEADS, PEER_TOPK * PEER_TOPK), PEER_TOPK)
    i1 = jnp.take_along_axis(i_half[:, :, 0], flat // PEER_TOPK, axis=-1)
    i2 = jnp.take_along_axis(i_half[:, :, 1], flat % PEER_TOPK, axis=-1)
    expert = i1 * N_KEYS + i2
    g = jax.nn.softmax(best, axis=-1)
    act = jnp.einsum('chkd,cd->chk', u_tab[expert], xc, preferred_element_type=jnp.float32)
    coeff = (g * jax.nn.gelu(act)).astype(v_tab.dtype)
    return jnp.einsum('chk,chkd->cd', coeff, v_tab[expert], preferred_element_type=jnp.float32).astype(xc.dtype)


def peer_ffn(h, w_q, sub_keys, u_tab, v_tab):
    B, T, D = h.shape
    n = B * T
    flat = jnp.pad(h.reshape(n, D), ((0, -n % PEER_CHUNK), (0, 0)))
    out = lax.map(lambda c: peer_chunk(c, w_q, sub_keys, u_tab, v_tab), flat.reshape(-1, PEER_CHUNK, D))
    return out.reshape(-1, D)[:n].reshape(B, T, D)


def mixer_front(x, norm_gain, w_in, q_gain, k_gain):
    B, T, _ = x.shape
    h = rms_norm(x, norm_gain)
    p = jnp.einsum('btd,de->bte', h, w_in)
    offs = np.cumsum(PROJ_SIZES)[:-1].tolist()
    (q, ck, cv, sk, sv, wk, wv, g, sbq, sbk, sbv, ga, gb) = jnp.split(p, offs, axis=-1)
    kv_shape = (B, T, NSA_KV, HEAD_DIM)
    q = rms_norm(q.reshape(B, T, NSA_KV, NSA_GROUP, HEAD_DIM), q_gain)
    cmp_kv = jnp.stack([ck.reshape(kv_shape), cv.reshape(kv_shape)], axis=2)
    sel_kv = jnp.stack([rms_norm(sk.reshape(kv_shape), k_gain[1]), sv.reshape(kv_shape)], axis=2)
    win_kv = jnp.stack([rms_norm(wk.reshape(kv_shape), k_gain[2]), wv.reshape(kv_shape)], axis=2)
    gates = jax.nn.sigmoid(g.astype(jnp.float32)).reshape(B, T, NSA_KV, NSA_GROUP, 3)
    sb_shape = (B, T, SB_HEADS, HEAD_DIM)
    sb_q = sbq.reshape(sb_shape)
    sb_kv = jnp.stack([sbk.reshape(sb_shape), sbv.reshape(sb_shape)], axis=2)
    return q, gates, cmp_kv, sel_kv, win_kv, sb_q, sb_kv, ga, gb


def merge_branches(x, nsa_out, sb_out, gate_a, gate_b, w_up_nsa, w_up_sb, w_out):
    a = jnp.einsum('bte,ed->btd', nsa_out, w_up_nsa, preferred_element_type=jnp.float32)
    b = jnp.einsum('bte,ed->btd', sb_out, w_up_sb, preferred_element_type=jnp.float32)
    mixed = jax.nn.sigmoid(gate_a.astype(jnp.float32)) * a + jax.nn.sigmoid(gate_b.astype(jnp.float32)) * b
    return x + jnp.einsum('btd,de->bte', mixed.astype(x.dtype), w_out)


def setup_inputs(seed: int = 0) -> dict:
    key = jax.random.key(seed)
    ks = jax.random.split(key, 22)
    n_pages = PAST_LEN // PAGE_SIZE
    n_used = DEC_BATCH * n_pages
    n_phys = n_used + max(1, n_used // 4)
    win_buf = min(WINDOW, PAST_LEN)
    L = DEPTH

    def nrm(k, shape, scale=1.0):
        return scale * jax.random.normal(k, shape, jnp.float32)

    def gain(k, shape):
        return 1.0 + 0.05 * jax.random.normal(k, shape, jnp.float32)

    return {
        'x_prompt': nrm(ks[0], (BATCH, SEQ, D_MODEL)),
        'x_sample': nrm(ks[1], (DEC_BATCH, DEC_SEQ, D_MODEL)),
        'cache_cmp_kv': nrm(ks[2], (L, n_phys, PAGE_SIZE, 2, NSA_KV, HEAD_DIM)),
        'cache_sel_kv': nrm(ks[3], (L, n_phys, PAGE_SIZE, 2, NSA_KV, HEAD_DIM)),
        'cache_sb_kv': nrm(ks[4], (L, n_phys, PAGE_SIZE, 2, SB_HEADS, HEAD_DIM)),
        'state_win_kv': nrm(ks[5], (L, DEC_BATCH, win_buf, 2, NSA_KV, HEAD_DIM)),
        'page_table': jax.random.permutation(ks[6], n_phys)[:n_used].reshape(DEC_BATCH, n_pages).astype(jnp.int32),
        'norm1_gain': gain(ks[7], (L, D_MODEL)),
        'w_in': nrm(ks[8], (L, D_MODEL, PROJ_W), D_MODEL ** -0.5),
        'nsa_q_gain': gain(ks[9], (L, HEAD_DIM)),
        'nsa_k_gain': gain(ks[10], (L, 3, HEAD_DIM)),
        'cmp_pe': nrm(ks[11], (L, 2, CMP_LEN, HEAD_DIM), 0.5),
        'cmp_w1': nrm(ks[12], (L, 2, CMP_LEN * HEAD_DIM, CMP_HID), (CMP_LEN * HEAD_DIM) ** -0.5),
        'cmp_w2': nrm(ks[13], (L, 2, CMP_HID, HEAD_DIM), CMP_HID ** -0.5),
        'w_up_nsa': nrm(ks[14], (L, NSA_W, D_MODEL), NSA_W ** -0.5),
        'w_up_sb': nrm(ks[15], (L, SB_W, D_MODEL), SB_W ** -0.5),
        'w_out': nrm(ks[16], (L, D_MODEL, D_MODEL), D_MODEL ** -0.5),
        'norm2_gain': gain(ks[17], (L, D_MODEL)),
        'peer_w_q': nrm(ks[18], (L, D_MODEL, PEER_HEADS * PEER_DK), D_MODEL ** -0.5),
        'peer_sub_keys': nrm(ks[19], (L, 2, N_KEYS, PEER_DK // 2), (PEER_DK // 2) ** -0.5),
        'peer_u': nrm(ks[20], (L, N_EXPERTS, D_MODEL), D_MODEL ** -0.5),
        'peer_v': nrm(ks[21], (L, N_EXPERTS, D_MODEL), 0.5),
    }


def reference(x_prompt, x_sample, cache_cmp_kv, cache_sel_kv, cache_sb_kv, state_win_kv, page_table,
              norm1_gain, w_in, nsa_q_gain, nsa_k_gain, cmp_pe, cmp_w1, cmp_w2, w_up_nsa, w_up_sb, w_out,
              norm2_gain, peer_w_q, peer_sub_keys, peer_u, peer_v):
    y_p, y_s = x_prompt, x_sample
    cmp_p, sel_p, sb_p, win_p = [], [], [], []
    cmp_s, sel_s, sb_s, win_s = [], [], [], []
    for layer in range(DEPTH):
        q, gates, cmp_kv, sel_kv, win_kv, sb_q, sb_kv, ga, gb = mixer_front(
            y_p, norm1_gain[layer], w_in[layer], nsa_q_gain[layer], nsa_k_gain[layer])
        a = nsa_prompt(q, gates, cmp_kv, sel_kv, win_kv, nsa_k_gain[layer, 0], cmp_pe[layer], cmp_w1[layer], cmp_w2[layer])
        b = sb_prompt(sb_q, sb_kv)
        y_p = merge_branches(y_p, a, b, ga, gb, w_up_nsa[layer], w_up_sb[layer], w_out[layer])
        y_p = y_p + peer_ffn(rms_norm(y_p, norm2_gain[layer]), peer_w_q[layer], peer_sub_keys[layer],
                             peer_u[layer], peer_v[layer])
        cmp_p.append(cmp_kv)
        sel_p.append(sel_kv)
        sb_p.append(sb_kv)
        win_p.append(win_kv[:, -min(WINDOW, win_kv.shape[1]):])
        q, gates, cmp_kv, sel_kv, win_kv, sb_q, sb_kv, ga, gb = mixer_front(
            y_s, norm1_gain[layer], w_in[layer], nsa_q_gain[layer], nsa_k_gain[layer])
        a, new_win = nsa_sample(q, gates, cmp_kv, sel_kv, win_kv, cache_cmp_kv, cache_sel_kv, state_win_kv[layer],
                                page_table, layer, nsa_k_gain[layer, 0], cmp_pe[layer], cmp_w1[layer], cmp_w2[layer])
        b = sb_sample(sb_q, sb_kv, cache_sb_kv, page_table, layer)
        y_s = merge_branches(y_s, a, b, ga, gb, w_up_nsa[layer], w_up_sb[layer], w_out[layer])
        y_s = y_s + peer_ffn(rms_norm(y_s, norm2_gain[layer]), peer_w_q[layer], peer_sub_keys[layer],
                             peer_u[layer], peer_v[layer])
        cmp_s.append(cmp_kv)
        sel_s.append(sel_kv)
        sb_s.append(sb_kv)
        win_s.append(new_win)
    return (y_p, y_s, jnp.stack(cmp_p), jnp.stack(sel_p), jnp.stack(sb_p), jnp.stack(win_p),
            jnp.stack(cmp_s), jnp.stack(sel_s), jnp.stack(sb_s), jnp.stack(win_s))
```

```python
import functools

import numpy as np
import jax
import jax.numpy as jnp
from jax import lax
from jax.experimental import pallas as pl
from jax.experimental.pallas import tpu as pltpu

D_MODEL = 2048
PAGE_SIZE = 128
HEAD_DIM = 128
NSA_HEADS = 8
NSA_KV = 2
NSA_GROUP = NSA_HEADS // NSA_KV
NSA_W = NSA_HEADS * HEAD_DIM
KV_W = NSA_KV * HEAD_DIM
CMP_LEN = 32
CMP_STRIDE = 16
CMP_HID = 256
SEL_LEN = 64
TOP_N = 16
WINDOW = 512
FORCE_SCORE = 1.0e4
SB_HEADS = 8
SB_W = SB_HEADS * HEAD_DIM
PEER_HEADS = 8
PEER_TOPK = 16
N_KEYS = 128
N_EXPERTS = N_KEYS * N_KEYS
PEER_DK = 128
NORM_EPS = 1e-6
GATE_W = 3 * NSA_HEADS
GATE_PAD = 128
ATT_SCALE = HEAD_DIM ** -0.5
SAMPLE_ROW_PAD = 128

F32 = jnp.float32
BF16 = jnp.bfloat16
NEG_INF = float("-inf")

VMEM_LIMIT_BYTES = 56 * 1024 * 1024


def _cparams(*sem):
    return pltpu.CompilerParams(dimension_semantics=sem, vmem_limit_bytes=VMEM_LIMIT_BYTES)


def _rms(x, gain):
    return x * lax.rsqrt(jnp.mean(x * x, axis=-1, keepdims=True) + NORM_EPS) * gain


def _gelu(x):
    return 0.5 * x * (1.0 + jnp.tanh(0.7978845608028654 * (x + 0.044715 * (x * x * x))))


def _softplus(z):
    return jnp.maximum(z, 0.0) + jnp.log1p(jnp.exp(-jnp.abs(z)))


def _masked_softmax(s, mask):
    sm = jnp.where(mask, s, NEG_INF)
    m = jnp.max(sm, axis=-1, keepdims=True)
    m = jnp.where(m > NEG_INF, m, 0.0)
    e = jnp.where(mask, jnp.exp(s - m), 0.0)
    den = jnp.sum(e, axis=-1, keepdims=True)
    return e / jnp.where(den > 0, den, 1.0)


def _dot(a, b):
    return jnp.dot(a, b, preferred_element_type=F32)


def _dot_nt(a, b):
    return lax.dot_general(a, b, (((1,), (1,)), ((), ())), preferred_element_type=F32)


def _row_tile(m, pref):
    return pref if m % pref == 0 else m


def _rmsnorm_kernel(x_ref, g_ref, o_ref):
    o_ref[...] = _rms(x_ref[...], g_ref[...]).astype(o_ref.dtype)


def _rmsnorm(x, gain):
    m, d = x.shape
    tm = _row_tile(m, 512)
    return pl.pallas_call(
        _rmsnorm_kernel,
        grid=(m // tm,),
        in_specs=[pl.BlockSpec((tm, d), lambda i: (i, 0)), pl.BlockSpec((1, d), lambda i: (0, 0))],
        out_specs=pl.BlockSpec((tm, d), lambda i: (i, 0)),
        out_shape=jax.ShapeDtypeStruct((m, d), BF16),
        compiler_params=_cparams("parallel"),
        name="rmsnorm1",
    )(x, gain.reshape(1, d))


def _proj_q_kernel(h_ref, w_ref, qg_ref, q_ref, sbq_ref):
    acc = _dot(h_ref[...], w_ref[...])
    for hd in range(NSA_HEADS):
        sl = slice(hd * HEAD_DIM, (hd + 1) * HEAD_DIM)
        q_ref[:, sl] = _rms(acc[:, sl], qg_ref[...]).astype(BF16)
    sbq_ref[...] = acc[:, NSA_W:].astype(BF16)


def _proj_q(h, w, q_gain):
    m, d = h.shape
    tm = _row_tile(m, 512)
    n = NSA_W + SB_W
    return pl.pallas_call(
        _proj_q_kernel,
        grid=(m // tm,),
        in_specs=[pl.BlockSpec((tm, d), lambda i: (i, 0)), pl.BlockSpec((d, n), lambda i: (0, 0)),
                  pl.BlockSpec((1, HEAD_DIM), lambda i: (0, 0))],
        out_specs=[pl.BlockSpec((tm, NSA_W), lambda i: (i, 0)), pl.BlockSpec((tm, SB_W), lambda i: (i, 0))],
        out_shape=[jax.ShapeDtypeStruct((m, NSA_W), BF16), jax.ShapeDtypeStruct((m, SB_W), BF16)],
        compiler_params=_cparams("parallel"),
        name="proj_q",
    )(h, w, q_gain.reshape(1, HEAD_DIM))


def _proj_kv_kernel(h_ref, w_ref, kg_ref, cmp_ref, sel_ref, win_ref, gate_ref, cmph_ref):
    acc = _dot(h_ref[...], w_ref[...])
    row_w = 2 * KV_W
    cmp_ref[...] = acc[:, :row_w]
    for c in range(2 * NSA_KV):
        cmph_ref[c] = acc[:, c * HEAD_DIM:(c + 1) * HEAD_DIM].astype(BF16)
    for r, out_ref in ((1, sel_ref), (2, win_ref)):
        base = r * row_w
        for g in range(NSA_KV):
            sl = slice(base + g * HEAD_DIM, base + (g + 1) * HEAD_DIM)
            out_ref[:, g * HEAD_DIM:(g + 1) * HEAD_DIM] = _rms(acc[:, sl], kg_ref[r:r + 1, :])
        out_ref[:, KV_W:] = acc[:, base + KV_W:base + row_w]
    gate_ref[...] = jax.nn.sigmoid(acc[:, 3 * row_w:])


def _proj_kv(h, w, k_gain):
    m, d = h.shape
    tm = _row_tile(m, 512)
    n = 6 * KV_W + GATE_PAD
    row_w = 2 * KV_W
    row_spec = pl.BlockSpec((tm, row_w), lambda i: (i, 0))
    return pl.pallas_call(
        _proj_kv_kernel,
        grid=(m // tm,),
        in_specs=[pl.BlockSpec((tm, d), lambda i: (i, 0)), pl.BlockSpec((d, n), lambda i: (0, 0)),
                  pl.BlockSpec((3, HEAD_DIM), lambda i: (0, 0))],
        out_specs=[row_spec, row_spec, row_spec, pl.BlockSpec((tm, GATE_PAD), lambda i: (i, 0)),
                   pl.BlockSpec((2 * NSA_KV, tm, HEAD_DIM), lambda i: (0, i, 0))],
        out_shape=[jax.ShapeDtypeStruct((m, row_w), F32)] * 3
        + [jax.ShapeDtypeStruct((m, GATE_PAD), F32), jax.ShapeDtypeStruct((2 * NSA_KV, m, HEAD_DIM), BF16)],
        compiler_params=_cparams("parallel"),
        name="proj_kv",
    )(h, w, k_gain)


def _proj_sbkv_kernel(h_ref, w_ref, o_ref):
    o_ref[...] = _dot(h_ref[...], w_ref[...])


def _proj_sbkv(h, w):
    m, d = h.shape
    tm = _row_tile(m, 512)
    n = 2 * SB_W
    return pl.pallas_call(
        _proj_sbkv_kernel,
        grid=(m // tm,),
        in_specs=[pl.BlockSpec((tm, d), lambda i: (i, 0)), pl.BlockSpec((d, n), lambda i: (0, 0))],
        out_specs=pl.BlockSpec((tm, n), lambda i: (i, 0)),
        out_shape=jax.ShapeDtypeStruct((m, n), F32),
        compiler_params=_cparams("parallel"),
        name="proj_sbkv",
    )(h, w)


def _mix_kernel(h_ref, a_ref, b_ref, wga_ref, wgb_ref, wa_ref, wb_ref, o_ref):
    h = h_ref[...]
    ga = jax.nn.sigmoid(_dot(h, wga_ref[...]))
    gb = jax.nn.sigmoid(_dot(h, wgb_ref[...]))
    mixed = ga * _dot(a_ref[...], wa_ref[...]) + gb * _dot(b_ref[...], wb_ref[...])
    o_ref[...] = mixed.astype(o_ref.dtype)


def _mix(h, nsa_out, sb_out, w_ga, w_gb, w_up_nsa, w_up_sb):
    m, d = h.shape
    tm = _row_tile(m, 512)
    tn = 512
    return pl.pallas_call(
        _mix_kernel,
        grid=(d // tn, m // tm),
        in_specs=[pl.BlockSpec((tm, d), lambda j, i: (i, 0)),
                  pl.BlockSpec((tm, NSA_W), lambda j, i: (i, 0)),
                  pl.BlockSpec((tm, SB_W), lambda j, i: (i, 0)),
                  pl.BlockSpec((d, tn), lambda j, i: (0, j)),
                  pl.BlockSpec((d, tn), lambda j, i: (0, j)),
                  pl.BlockSpec((NSA_W, tn), lambda j, i: (0, j)),
                  pl.BlockSpec((SB_W, tn), lambda j, i: (0, j))],
        out_specs=pl.BlockSpec((tm, tn), lambda j, i: (i, j)),
        out_shape=jax.ShapeDtypeStruct((m, d), BF16),
        compiler_params=_cparams("parallel", "parallel"),
        name="mix",
    )(h, nsa_out, sb_out, w_ga, w_gb, w_up_nsa, w_up_sb)


def _out_proj_kernel(x_ref, mixed_ref, w_ref, g_ref, y_ref, hn_ref):
    y = x_ref[...] + _dot(mixed_ref[...], w_ref[...])
    y_ref[...] = y
    hn_ref[...] = _rms(y, g_ref[...]).astype(BF16)


def _out_proj(x, mixed, w_out, norm2_gain):
    m, d = x.shape
    tm = _row_tile(m, 512)
    row = pl.BlockSpec((tm, d), lambda i: (i, 0))
    return pl.pallas_call(
        _out_proj_kernel,
        grid=(m // tm,),
        in_specs=[row, row, pl.BlockSpec((d, d), lambda i: (0, 0)), pl.BlockSpec((1, d), lambda i: (0, 0))],
        out_specs=[row, row],
        out_shape=[jax.ShapeDtypeStruct((m, d), F32), jax.ShapeDtypeStruct((m, d), BF16)],
        compiler_params=_cparams("parallel"),
        name="out_proj",
    )(x, mixed, w_out, norm2_gain.reshape(1, d))


def _topk_rows(s, k, payloads=()):
    n = s.shape[0]
    iota = lax.broadcasted_iota(jnp.int32, s.shape, 0).astype(F32)
    vals, idxs = [], []
    picked = [[] for _ in payloads]
    for _ in range(k):
        m = jnp.max(s, axis=0, keepdims=True)
        idx = jnp.min(jnp.where(s == m, iota, float(n)), axis=0, keepdims=True)
        hit = iota == idx
        vals.append(m)
        idxs.append(idx)
        for p, acc in zip(payloads, picked):
            acc.append(jnp.sum(jnp.where(hit, p, 0.0), axis=0, keepdims=True))
        s = jnp.where(hit, NEG_INF, s)
    cat = lambda xs: jnp.concatenate(xs, axis=0)
    return cat(vals), cat(idxs), [cat(p) for p in picked]


_PEER_PAIRS = [(a, b) for a in range(PEER_TOPK) for b in range(PEER_TOPK) if (a + 1) * (b + 1) <= PEER_TOPK]


def _peer_route_kernel(hn_ref, wq_ref, keys_ref, g_ref, ia_ref, ib_ref):
    q = _dot(hn_ref[...], wq_ref[...]).astype(BF16)
    g_rows, a_rows, b_rows = [], [], []
    for hd in range(PEER_HEADS):
        qh = q[:, hd * PEER_DK:(hd + 1) * PEER_DK]
        s = _dot_nt(keys_ref[...], qh)
        s1, i1, _ = _topk_rows(s[:N_KEYS], PEER_TOPK)
        s2, i2, _ = _topk_rows(s[N_KEYS:], PEER_TOPK)
        cand = jnp.concatenate([s1[a:a + 1] + s2[b:b + 1] for a, b in _PEER_PAIRS], axis=0)
        pa = jnp.concatenate([i1[a:a + 1] for a, _ in _PEER_PAIRS], axis=0)
        pb = jnp.concatenate([i2[b:b + 1] for _, b in _PEER_PAIRS], axis=0)
        best, _, (ea, eb) = _topk_rows(cand, PEER_TOPK, (pa, pb))
        e = jnp.exp(best - best[0:1])
        g_rows.append(e / jnp.sum(e, axis=0, keepdims=True))
        a_rows.append(ea)
        b_rows.append(eb)
    g_ref[...] = jnp.concatenate(g_rows, axis=0).T
    ia_ref[...] = jnp.concatenate(a_rows, axis=0).T
    ib_ref[...] = jnp.concatenate(b_rows, axis=0).T


def _peer_route(hn, w_q, keys_bd):
    m, d = hn.shape
    tm = _row_tile(m, 256)
    n_slot = PEER_HEADS * PEER_TOPK
    slot = pl.BlockSpec((tm, n_slot), lambda i: (i, 0))
    return pl.pallas_call(
        _peer_route_kernel,
        grid=(m // tm,),
        in_specs=[pl.BlockSpec((tm, d), lambda i: (i, 0)),
                  pl.BlockSpec((d, PEER_HEADS * PEER_DK), lambda i: (0, 0)),
                  pl.BlockSpec((2 * N_KEYS, PEER_DK), lambda i: (0, 0))],
        out_specs=[slot, slot, slot],
        out_shape=[jax.ShapeDtypeStruct((m, n_slot), F32)] * 3,
        compiler_params=_cparams("parallel"),
        name="peer_route",
    )(hn, w_q, keys_bd)


def _peer_expand_kernel(g_ref, ia_ref, ib_ref, w_ref):
    tm = g_ref.shape[0]
    row_id = lax.broadcasted_iota(jnp.int32, (N_KEYS, g_ref.shape[1]), 0).astype(F32)

    def body(t, carry):
        g = g_ref[pl.ds(t, 1), :]
        rt = jnp.where(row_id == ia_ref[pl.ds(t, 1), :], g, 0.0).astype(BF16)
        ct = jnp.where(row_id == ib_ref[pl.ds(t, 1), :], 1.0, 0.0).astype(BF16)
        w_ref[t] = _dot_nt(rt, ct).astype(w_ref.dtype)
        return carry

    lax.fori_loop(0, tm, body, 0)


def _peer_expand(g, ia, ib):
    m, n_slot = g.shape
    tm = _row_tile(m, 128)
    slot = pl.BlockSpec((tm, n_slot), lambda i: (i, 0))
    return pl.pallas_call(
        _peer_expand_kernel,
        grid=(m // tm,),
        in_specs=[slot, slot, slot],
        out_specs=pl.BlockSpec((tm, N_KEYS, N_KEYS), lambda i: (i, 0, 0)),
        out_shape=jax.ShapeDtypeStruct((m, N_KEYS, N_KEYS), BF16),
        compiler_params=_cparams("parallel"),
        name="peer_expand",
    )(g, ia, ib)


def _peer_dense_kernel(y_ref, hn_ref, w_ref, u_ref, v_ref, o_ref):
    @pl.when(pl.program_id(1) == 0)
    def _():
        o_ref[...] = y_ref[...]

    act = _dot_nt(hn_ref[...], u_ref[...])
    coeff = (w_ref[...].astype(F32) * _gelu(act)).astype(BF16)
    o_ref[...] += _dot(coeff, v_ref[...])


def _peer_dense(y, hn, w, u_tab, v_tab):
    m, d = y.shape
    tm = _row_tile(m, 1024)
    te = 512
    row = pl.BlockSpec((tm, d), lambda i, j: (i, 0))
    tab = pl.BlockSpec((te, d), lambda i, j: (j, 0))
    return pl.pallas_call(
        _peer_dense_kernel,
        grid=(m // tm, N_EXPERTS // te),
        in_specs=[row, row, pl.BlockSpec((tm, te), lambda i, j: (i, j)), tab, tab],
        out_specs=row,
        out_shape=jax.ShapeDtypeStruct((m, d), F32),
        compiler_params=_cparams("parallel", "arbitrary"),
        name="peer_dense",
    )(y, hn, w, u_tab, v_tab)


def _peer(y, hn, w_q, keys_bd, u_tab, v_tab):
    g, ia, ib = _peer_route(hn, w_q, keys_bd)
    w = _peer_expand(g, ia, ib).reshape(y.shape[0], N_EXPERTS)
    return _peer_dense(y, hn, w, u_tab, v_tab)


def _compress_kernel(ch_ref, w1a_ref, w1b_ref, pe_ref, w2_ref, kg_ref, o_ref):
    ch = ch_ref[...]
    n_chunk = ch.shape[0]
    half = w1a_ref.shape[0]
    pe = pe_ref[...]
    bias = _dot(pe[:, :half], w1a_ref[...]) + _dot(pe[:, half:], w1b_ref[...])
    nxt = pltpu.roll(_dot(ch, w1b_ref[...]), n_chunk - 1, 0)
    pre = _dot(ch, w1a_ref[...]) + nxt + bias[0:1]
    out = _dot(_gelu(pre).astype(BF16), w2_ref[...])
    is_key = pl.program_id(0) < NSA_KV
    o_ref[...] = jnp.where(is_key, _rms(out, kg_ref[...]), out)


def _compress(rows_hm, pe, w1, w2, k_gain, n_seq):
    n_c, total, d = rows_hm.shape
    n_chunk = total // n_seq // CMP_STRIDE
    half = CMP_STRIDE * d
    ch = rows_hm.reshape(n_c, n_seq, n_chunk, half)
    pe_flat = jnp.broadcast_to(pe.reshape(2, 1, 2 * half), (2, 8, 2 * half)).astype(BF16)
    w1 = w1.astype(BF16)
    kv = lambda c, b: (c // NSA_KV, 0, 0)
    return pl.pallas_call(
        _compress_kernel,
        grid=(n_c, n_seq),
        in_specs=[pl.BlockSpec((None, None, n_chunk, half), lambda c, b: (c, b, 0, 0)),
                  pl.BlockSpec((None, half, CMP_HID), kv),
                  pl.BlockSpec((None, half, CMP_HID), lambda c, b: (c // NSA_KV, 1, 0)),
                  pl.BlockSpec((None, 8, 2 * half), kv),
                  pl.BlockSpec((None, CMP_HID, d), kv),
                  pl.BlockSpec((1, d), lambda c, b: (0, 0))],
        out_specs=pl.BlockSpec((None, None, n_chunk, d), lambda c, b: (c, b, 0, 0)),
        out_shape=jax.ShapeDtypeStruct((n_c, n_seq, n_chunk, d), F32),
        compiler_params=_cparams("parallel", "parallel"),
        name="compress",
    )(ch, w1, w1, pe_flat, w2.astype(BF16), k_gain.reshape(1, d))


def _split3(x):
    a = x.astype(BF16)
    r = x - a.astype(F32)
    b = r.astype(BF16)
    c = (r - b.astype(F32)).astype(BF16)
    return a, b, c


def _nsa_prompt_kernel(q_ref, gate_ref, slope_ref, kc_ref, vc_ref, ks_ref, vs_ref, kw_ref, vw_ref, o_ref, *, tq):
    g = pl.program_id(1)
    t0 = pl.program_id(2) * tq
    seq = ks_ref.shape[0]
    n_cmp = kc_ref.shape[0]
    n_sel = seq // SEL_LEN
    rows = NSA_GROUP * tq
    q4 = jnp.concatenate([q_ref[:, j * HEAD_DIM:(j + 1) * HEAD_DIM] for j in range(NSA_GROUP)], axis=0)
    slope = slope_ref[...]
    tpos = t0 + lax.broadcasted_iota(jnp.int32, (tq, 1), 0)
    tpos4 = jnp.concatenate([tpos] * NSA_GROUP, axis=0)

    def attend(k, v, key_pos, mask):
        s = _dot_nt(q4, k.astype(BF16)) * ATT_SCALE - slope * (tpos4 - key_pos).astype(F32)
        p = _masked_softmax(s, mask)
        return p, _dot(p.astype(BF16), v.astype(BF16))

    c_end = lax.broadcasted_iota(jnp.int32, (1, n_cmp), 1) * CMP_STRIDE + (CMP_LEN - 1)
    p_c, o_c = attend(kc_ref[...], vc_ref[...], c_end, c_end <= tpos4)

    p_sum = p_c[0:tq]
    for j in range(1, NSA_GROUP):
        p_sum = p_sum + p_c[j * tq:(j + 1) * tq]
    ci = lax.broadcasted_iota(jnp.int32, (n_cmp, 128), 0)
    bj = lax.broadcasted_iota(jnp.int32, (n_cmp, 128), 1)
    ratio = SEL_LEN // CMP_STRIDE
    overlap = jnp.where((ci >= ratio * bj - 1) & (ci <= ratio * bj + ratio - 1) & (bj < n_sel), 1.0, 0.0).astype(BF16)
    imp = sum(_dot(part, overlap) for part in _split3(p_sum))

    blk = lax.broadcasted_iota(jnp.int32, (tq, 128), 1)
    cur = tpos // SEL_LEN
    forced = (blk == 0) | (blk == cur) | (blk == cur - 1)
    score = jnp.where(forced, FORCE_SCORE, jnp.where(blk * SEL_LEN <= tpos, imp, -1.0))
    score = jnp.where(blk < n_sel, score, -2.0)
    rank = jnp.zeros((tq, 128), F32)
    for i in range(n_sel):
        col = score[:, i:i + 1]
        rank = rank + jnp.where((col > score) | ((col == score) & (blk > i)), 1.0, 0.0)
    chosen = jnp.where((rank < float(min(TOP_N, n_sel))) & (blk < n_sel), 1.0, 0.0).astype(BF16)
    bi = lax.broadcasted_iota(jnp.int32, (128, seq), 0)
    kk = lax.broadcasted_iota(jnp.int32, (128, seq), 1)
    expand = jnp.where(kk // SEL_LEN == bi, 1.0, 0.0).astype(BF16)
    sel_keys = _dot(chosen, expand)
    sel4 = jnp.concatenate([sel_keys] * NSA_GROUP, axis=0)

    key_pos = lax.broadcasted_iota(jnp.int32, (1, seq), 1)
    _, o_s = attend(ks_ref[...], vs_ref[...], key_pos, (sel4 > 0.5) & (key_pos <= tpos4))

    span = min(WINDOW + tq, seq)
    k0 = pl.multiple_of(jnp.clip(t0 - WINDOW, 0, seq - span), 8)
    win_pos = k0 + lax.broadcasted_iota(jnp.int32, (1, span), 1)
    dist = tpos4 - win_pos
    _, o_w = attend(kw_ref[pl.ds(k0, span), :], vw_ref[pl.ds(k0, span), :], win_pos, (dist >= 0) & (dist <= WINDOW))

    gates = jnp.where(g == 0, gate_ref[:, 0:3 * NSA_GROUP], gate_ref[:, 3 * NSA_GROUP:6 * NSA_GROUP])
    for j in range(NSA_GROUP):
        r = slice(j * tq, (j + 1) * tq)
        out = (gates[:, 3 * j:3 * j + 1] * o_c[r] + gates[:, 3 * j + 1:3 * j + 2] * o_s[r]
               + gates[:, 3 * j + 2:3 * j + 3] * o_w[r])
        o_ref[:, j * HEAD_DIM:(j + 1) * HEAD_DIM] = out.astype(o_ref.dtype)


def _alibi_rows(tq):
    h = np.arange(1, NSA_HEADS + 1, dtype=np.float32)
    s = np.exp2(-8.0 * h / NSA_HEADS).reshape(NSA_KV, NSA_GROUP)
    return jnp.asarray(np.repeat(s, tq, axis=1)[..., None])


def _nsa_prompt(q, gates, kcvc, sel_rows, win_rows, n_seq):
    total = q.shape[0]
    seq = total // n_seq
    tq = 64
    nq = seq // tq
    n_cmp = kcvc.shape[2]
    gw = NSA_GROUP * HEAD_DIM
    kv_k = pl.BlockSpec((seq, HEAD_DIM), lambda b, g, i: (b, g))
    kv_v = pl.BlockSpec((seq, HEAD_DIM), lambda b, g, i: (b, NSA_KV + g))
    return pl.pallas_call(
        functools.partial(_nsa_prompt_kernel, tq=tq),
        grid=(n_seq, NSA_KV, nq),
        in_specs=[pl.BlockSpec((tq, gw), lambda b, g, i: (b * nq + i, g)),
                  pl.BlockSpec((tq, GATE_PAD), lambda b, g, i: (b * nq + i, 0)),
                  pl.BlockSpec((None, NSA_GROUP * tq, 1), lambda b, g, i: (g, 0, 0)),
                  pl.BlockSpec((None, None, n_cmp, HEAD_DIM), lambda b, g, i: (g, b, 0, 0)),
                  pl.BlockSpec((None, None, n_cmp, HEAD_DIM), lambda b, g, i: (NSA_KV + g, b, 0, 0)),
                  kv_k, kv_v, kv_k, kv_v],
        out_specs=pl.BlockSpec((tq, gw), lambda b, g, i: (b * nq + i, g)),
        out_shape=jax.ShapeDtypeStruct((total, NSA_W), BF16),
        compiler_params=_cparams("parallel", "parallel", "arbitrary"),
        name="nsa_prompt",
    )(q, gates, _alibi_rows(tq), kcvc, kcvc, sel_rows, sel_rows, win_rows, win_rows)


def _sb_prompt_kernel(q_ref, k_ref, v_ref, o_ref, *, tq):
    i = pl.program_id(2)
    q = q_ref[...]
    t_pos = i * tq + lax.broadcasted_iota(jnp.int32, (tq, 1), 0)
    col = lax.broadcasted_iota(jnp.int32, (1, tq), 1)
    si = lax.broadcasted_iota(jnp.int32, (tq, tq), 0)
    sj = lax.broadcasted_iota(jnp.int32, (tq, tq), 1)
    after = jnp.where(si > sj, 1.0, 0.0).astype(BF16)

    def body(jj, state):
        carry, acc = state
        j = i - jj
        k0 = pl.multiple_of(j * tq, tq)
        z = _dot_nt(q, k_ref[pl.ds(k0, tq), :].astype(BF16)) * ATT_SCALE
        mask = (k0 + col) < t_pos
        log_keep_all = -_softplus(z)
        log_keep = jnp.where(mask, log_keep_all, 0.0)
        later = sum(_dot(part, after) for part in _split3(log_keep))
        a = jnp.where(mask, jnp.exp(z + log_keep_all + later + carry), 0.0)
        acc = acc + _dot(a.astype(BF16), v_ref[pl.ds(k0, tq), :].astype(BF16))
        return carry + jnp.sum(log_keep, axis=-1, keepdims=True), acc

    _, acc = lax.fori_loop(0, i + 1, body, (jnp.zeros((tq, 1), F32), jnp.zeros((tq, HEAD_DIM), F32)))
    o_ref[...] = acc.astype(o_ref.dtype)


def _sb_prompt(q, sb_rows, n_seq):
    total = q.shape[0]
    seq = total // n_seq
    tq = 128
    nq = seq // tq
    return pl.pallas_call(
        functools.partial(_sb_prompt_kernel, tq=tq),
        grid=(n_seq, SB_HEADS, nq),
        in_specs=[pl.BlockSpec((tq, HEAD_DIM), lambda b, h, i: (b * nq + i, h)),
                  pl.BlockSpec((seq, HEAD_DIM), lambda b, h, i: (b, h)),
                  pl.BlockSpec((seq, HEAD_DIM), lambda b, h, i: (b, SB_HEADS + h))],
        out_specs=pl.BlockSpec((tq, HEAD_DIM), lambda b, h, i: (b * nq + i, h)),
        out_shape=jax.ShapeDtypeStruct((total, SB_W), BF16),
        compiler_params=_cparams("parallel", "parallel", "arbitrary"),
        name="sb_prompt",
    )(q, sb_rows, sb_rows)


def _j_rms(x, gain):
    return x * lax.rsqrt(jnp.mean(x * x, axis=-1, keepdims=True) + NORM_EPS) * gain


def _j_masked_softmax(s, mask, axis=-1):
    s = jnp.where(mask, s, -jnp.inf)
    m = jnp.max(s, axis=axis, keepdims=True)
    m = jnp.where(jnp.isfinite(m), m, 0.0)
    e = jnp.where(mask, jnp.exp(s - m), 0.0)
    den = jnp.sum(e, axis=axis, keepdims=True)
    return e / jnp.where(den > 0, den, 1.0)


def _j_compress_rows(rows, pe, w1, w2):
    B, L, G, d = rows.shape
    n_chunk = L // CMP_STRIDE
    half = CMP_STRIDE * d
    ch = rows.reshape(B, n_chunk, CMP_STRIDE, G, d).transpose(0, 1, 3, 2, 4).reshape(B, n_chunk, G, half)
    pre = (jnp.einsum('bngf,fh->bngh', ch[:, :-1], w1[:half]) + jnp.einsum('bngf,fh->bngh', ch[:, 1:], w1[half:])
           + pe.reshape(-1) @ w1)
    return jnp.einsum('bngh,hd->bngd', jax.nn.gelu(pre), w2)


def _j_nsa_sample(q, gates, cmp_new, sel_new, win_new, cache_cmp_kv, cache_sel_kv, win_state, page_table,
                  k_gain_c, pe, w1, w2):
    B, T = q.shape[:2]
    past = page_table.shape[1] * PAGE_SIZE
    L = past + T
    L_pad = -(-L // SEL_LEN) * SEL_LEN
    cmp_past = cache_cmp_kv[page_table].reshape(B, past, 2, NSA_KV, HEAD_DIM)
    cmp_all = jnp.pad(jnp.concatenate([cmp_past, cmp_new], axis=1), ((0, 0), (0, L_pad - L), (0, 0), (0, 0), (0, 0)))
    kc = _j_rms(_j_compress_rows(cmp_all[:, :, 0], pe[0], w1[0], w2[0]), k_gain_c)
    vc = _j_compress_rows(cmp_all[:, :, 1], pe[1], w1[1], w2[1])
    c_end = jnp.arange(kc.shape[1]) * CMP_STRIDE + (CMP_LEN - 1)
    n_past_blk = past // SEL_LEN
    n_new_blk = L_pad // SEL_LEN - n_past_blk
    blk_per_page = PAGE_SIZE // SEL_LEN
    new_blocks = jnp.pad(sel_new, ((0, 0), (0, n_new_blk * SEL_LEN - T), (0, 0), (0, 0), (0, 0)))
    new_blocks = new_blocks.reshape(B, n_new_blk, SEL_LEN, 2, NSA_KV, HEAD_DIM)
    bidx = jnp.arange(B)[:, None, None, None]
    gidx = jnp.arange(NSA_KV)[None, None, :, None]
    band = jnp.concatenate([win_state, win_new], axis=1)
    wb = win_state.shape[1]
    band_pos = past - wb + jnp.arange(wb + T)
    q_pos = past + jnp.arange(T)
    scale = HEAD_DIM ** -0.5
    h = jnp.arange(1, NSA_HEADS + 1, dtype=F32)
    slopes = jnp.exp2(-8.0 * h / NSA_HEADS).reshape(NSA_KV, NSA_GROUP)[None, None, :, :, None]
    tq = q_pos[:, None]
    s_c = jnp.einsum('btgjd,bngd->btgjn', q, kc) * scale
    s_c = s_c - slopes * (tq - c_end[None, :]).astype(F32)[None, :, None, None, :]
    p_c = _j_masked_softmax(s_c, (c_end[None, :] <= tq)[None, :, None, None, :])
    o_c = jnp.einsum('btgjn,bngd->btgjd', p_c, vc)
    p = p_c.sum(axis=3)
    ratio = SEL_LEN // CMP_STRIDE
    nsel = (p.shape[-1] + 1) // ratio
    pad = jnp.pad(p, [(0, 0)] * (p.ndim - 1) + [(1, 1)])
    imp = pad[..., :ratio * nsel].reshape(p.shape[:-1] + (nsel, ratio)).sum(-1) + pad[..., ratio::ratio]
    blk = jnp.arange(nsel)
    cur = (q_pos // SEL_LEN)[:, None]
    valid = (blk * SEL_LEN)[None, :] <= tq
    forced = (blk == 0)[None, :] | (blk[None, :] == cur) | (blk[None, :] == cur - 1)
    score = jnp.where(forced[None, :, None, :], FORCE_SCORE, jnp.where(valid[None, :, None, :], imp, -1.0))
    _, idx = lax.top_k(score, min(TOP_N, nsel))
    jp = jnp.minimum(idx, n_past_blk - 1)
    page = page_table[bidx, jp // blk_per_page]
    row = (jp % blk_per_page)[..., None] * SEL_LEN + jnp.arange(SEL_LEN)
    jn = jnp.clip(idx - n_past_blk, 0, n_new_blk - 1)
    is_new = (idx >= n_past_blk)[..., None, None]
    k_past = cache_sel_kv[page[..., None], row, 0, gidx[..., None]]
    v_past = cache_sel_kv[page[..., None], row, 1, gidx[..., None]]
    k_s = jnp.where(is_new, new_blocks[bidx, jn, :, 0, gidx], k_past)
    v_s = jnp.where(is_new, new_blocks[bidx, jn, :, 1, gidx], v_past)
    dist_s = q_pos[None, :, None, None, None] - (idx[..., None] * SEL_LEN + jnp.arange(SEL_LEN))
    s_s = jnp.einsum('btgjd,btgnpd->btgjnp', q, k_s) * scale
    s_s = s_s - slopes[..., None] * dist_s[:, :, :, None].astype(F32)
    p_s = _j_masked_softmax(s_s, (dist_s >= 0)[:, :, :, None], axis=(-2, -1))
    o_s = jnp.einsum('btgjnp,btgnpd->btgjd', p_s, v_s)
    dist_w = tq - band_pos[None, :]
    mask_w = (dist_w >= 0) & (dist_w <= WINDOW) & (band_pos >= 0)[None, :]
    s_w = jnp.einsum('btgjd,bwgd->btgjw', q, band[:, :, 0]) * scale
    s_w = s_w - slopes * dist_w.astype(F32)[None, :, None, None, :]
    p_w = _j_masked_softmax(s_w, mask_w[None, :, None, None, :])
    o_w = jnp.einsum('btgjw,bwgd->btgjd', p_w, band[:, :, 1])
    out = gates[..., 0:1] * o_c + gates[..., 1:2] * o_s + gates[..., 2:3] * o_w
    return out.reshape(B, T, NSA_W), band[:, -wb:]


def _j_stick_break(z, mask, carry):
    log_keep = jnp.where(mask, jax.nn.log_sigmoid(-z), 0.0)
    later = lax.cumsum(log_keep, axis=log_keep.ndim - 1, reverse=True) - log_keep
    a = jnp.where(mask, jnp.exp(jax.nn.log_sigmoid(z) + later + carry[..., None]), 0.0)
    return a, carry + jnp.sum(log_keep, axis=-1)


def _j_sb_sample(q, kv_new, cache_sb_kv, page_table):
    B, T = q.shape[:2]
    scale = HEAD_DIM ** -0.5
    t_pos = jnp.arange(T)
    z = jnp.einsum('bthd,bshd->bhts', q, kv_new[:, :, 0]) * scale
    a, carry = _j_stick_break(z, t_pos[None, :] < t_pos[:, None], jnp.zeros((B, SB_HEADS, T), F32))
    o = jnp.einsum('bhts,bshd->bthd', a, kv_new[:, :, 1])

    def page_step(state, phys):
        carry, o = state
        kv = cache_sb_kv[phys]
        z = jnp.einsum('bthd,bshd->bhts', q, kv[:, :, 0]) * scale
        a, carry = _j_stick_break(z, True, carry)
        o = o + jnp.einsum('bhts,bshd->bthd', a, kv[:, :, 1])
        return (carry, o), None

    (carry, o), _ = lax.scan(page_step, (carry, o), page_table[:, ::-1].T)
    return o.reshape(B, T, SB_W)


def _split_w_in(w_in):
    w = w_in.astype(BF16)
    sizes = (NSA_W, KV_W, KV_W, KV_W, KV_W, KV_W, KV_W, GATE_W, SB_W, SB_W, SB_W, D_MODEL, D_MODEL)
    offs = np.concatenate([[0], np.cumsum(sizes)]).tolist()
    col = lambda a, b: w[:, offs[a]:offs[b]]
    w_q = jnp.concatenate([col(0, 1), col(8, 9)], axis=1)
    w_kv = jnp.concatenate([col(1, 7), jnp.pad(col(7, 8), ((0, 0), (0, GATE_PAD - GATE_W)))], axis=1)
    return w_q, w_kv, col(9, 11), col(11, 12), col(12, 13)


def _peer_keys(sub_keys):
    half = PEER_DK // 2
    z = jnp.zeros((N_KEYS, half), sub_keys.dtype)
    return jnp.concatenate([jnp.concatenate([sub_keys[0], z], axis=1),
                            jnp.concatenate([z, sub_keys[1]], axis=1)], axis=0).astype(BF16)


def kernel(x_prompt, x_sample, cache_cmp_kv, cache_sel_kv, cache_sb_kv, state_win_kv, page_table, norm1_gain, w_in,
           nsa_q_gain, nsa_k_gain, cmp_pe, cmp_w1, cmp_w2, w_up_nsa, w_up_sb, w_out, norm2_gain, peer_w_q,
           peer_sub_keys, peer_u, peer_v):
    assert w_in.shape[0] == 1, "single layer"
    B, T, D = x_prompt.shape
    S, TS, _ = x_sample.shape
    w_q, w_kv, w_sbkv, w_ga, w_gb = _split_w_in(w_in[0])
    w_un, w_us, w_o = w_up_nsa[0].astype(BF16), w_up_sb[0].astype(BF16), w_out[0].astype(BF16)
    wq_peer, keys_bd = peer_w_q[0].astype(BF16), _peer_keys(peer_sub_keys[0])
    u_tab, v_tab = peer_u[0].astype(BF16), peer_v[0].astype(BF16)
    k_gain = nsa_k_gain[0]

    def front(x2d):
        h = _rmsnorm(x2d, norm1_gain[0])
        q, sbq = _proj_q(h, w_q, nsa_q_gain[0])
        cmp_rows, sel_rows, win_rows, gates, cmp_hm = _proj_kv(h, w_kv, k_gain)
        sb_rows = _proj_sbkv(h, w_sbkv)
        return h, q, sbq, cmp_rows, sel_rows, win_rows, gates, cmp_hm, sb_rows

    def back(x2d, h, nsa_out, sb_out):
        mixed = _mix(h, nsa_out, sb_out, w_ga, w_gb, w_un, w_us)
        y, hn = _out_proj(x2d, mixed, w_o, norm2_gain[0])
        return _peer(y, hn, wq_peer, keys_bd, u_tab, v_tab)

    xp = x_prompt.reshape(B * T, D)
    h, q, sbq, cmp_rows, sel_rows, win_rows, gates, cmp_hm, sb_rows = front(xp)
    kcvc = _compress(cmp_hm, cmp_pe[0], cmp_w1[0], cmp_w2[0], k_gain[0], B)
    nsa_out = _nsa_prompt(q, gates, kcvc, sel_rows, win_rows, B)
    sb_out = _sb_prompt(sbq, sb_rows, B)
    y_p = back(xp, h, nsa_out, sb_out).reshape(B, T, D)
    kv5 = lambda rows, n, t, heads: rows.reshape(1, n, t, 2, heads, HEAD_DIM)
    cmp_p, sel_p, sb_p = kv5(cmp_rows, B, T, NSA_KV), kv5(sel_rows, B, T, NSA_KV), kv5(sb_rows, B, T, SB_HEADS)
    win_p = kv5(win_rows, B, T, NSA_KV)[:, :, -min(WINDOW, T):]

    n_s = S * TS
    xs = jnp.pad(x_sample.reshape(n_s, D), ((0, -n_s % SAMPLE_ROW_PAD), (0, 0)))
    h, *rest = front(xs)
    q, sbq, cmp_rows, sel_rows, win_rows, gates, cmp_hm, sb_rows = (a[..., :n_s, :] for a in rest)
    cmp_s, sel_s, sb_s = kv5(cmp_rows, S, TS, NSA_KV), kv5(sel_rows, S, TS, NSA_KV), kv5(sb_rows, S, TS, SB_HEADS)
    nsa_s, new_win = _j_nsa_sample(
        q.astype(F32).reshape(S, TS, NSA_KV, NSA_GROUP, HEAD_DIM), gates[:, :GATE_W].reshape(S, TS, NSA_KV, NSA_GROUP, 3),
        cmp_s[0], sel_s[0], kv5(win_rows, S, TS, NSA_KV)[0], cache_cmp_kv[0], cache_sel_kv[0], state_win_kv[0],
        page_table, k_gain[0], cmp_pe[0], cmp_w1[0], cmp_w2[0])
    sb_out_s = _j_sb_sample(sbq.astype(F32).reshape(S, TS, SB_HEADS, HEAD_DIM), sb_s[0], cache_sb_kv[0], page_table)
    pad_rows = lambda a: jnp.pad(a, ((0, xs.shape[0] - n_s), (0, 0)))
    y_s = back(xs, h, pad_rows(nsa_s.reshape(n_s, NSA_W).astype(BF16)), pad_rows(sb_out_s.reshape(n_s, SB_W).astype(BF16)))
    y_s = y_s[:n_s].reshape(S, TS, D)
    return (y_p, y_s, cmp_p, sel_p, sb_p, win_p, cmp_s, sel_s, sb_s, new_win[None])
```

```python
import functools

import numpy as np
import jax
import jax.numpy as jnp
from jax import lax
from jax.experimental import pallas as pl
from jax.experimental.pallas import tpu as pltpu

D_MODEL = 2048
PAGE_SIZE = 128
HEAD_DIM = 128
NSA_HEADS = 8
NSA_KV = 2
NSA_GROUP = NSA_HEADS // NSA_KV
NSA_W = NSA_HEADS * HEAD_DIM
KV_W = NSA_KV * HEAD_DIM
CMP_LEN = 32
CMP_STRIDE = 16
CMP_HID = 256
SEL_LEN = 64
TOP_N = 16
WINDOW = 512
FORCE_SCORE = 1.0e4
SB_HEADS = 8
SB_W = SB_HEADS * HEAD_DIM
PEER_HEADS = 8
PEER_TOPK = 16
N_KEYS = 128
N_EXPERTS = N_KEYS * N_KEYS
PEER_DK = 128
NORM_EPS = 1e-6
GATE_W = 3 * NSA_HEADS
GATE_PAD = 128
ATT_SCALE = HEAD_DIM ** -0.5
SAMPLE_ROW_PAD = 128
SEL_PREFIX_STEP = 512
SB_BLOCK = 128
EXP_UNDERFLOW = -104.0
CMP_PAGES_PER_STEP = 32

F32 = jnp.float32
BF16 = jnp.bfloat16
NEG_INF = float("-inf")

VMEM_LIMIT_BYTES = 56 * 1024 * 1024


def _cparams(*sem):
    return pltpu.CompilerParams(dimension_semantics=sem, vmem_limit_bytes=VMEM_LIMIT_BYTES)


def _rms(x, gain):
    return x * lax.rsqrt(jnp.mean(x * x, axis=-1, keepdims=True) + NORM_EPS) * gain


def _gelu(x):
    return 0.5 * x * (1.0 + jnp.tanh(0.7978845608028654 * (x + 0.044715 * (x * x * x))))


def _masked_softmax(s, mask):
    sm = jnp.where(mask, s, NEG_INF)
    m = jnp.max(sm, axis=-1, keepdims=True)
    m = jnp.where(m > NEG_INF, m, 0.0)
    e = jnp.where(mask, jnp.exp(s - m), 0.0)
    den = jnp.sum(e, axis=-1, keepdims=True)
    return e / jnp.where(den > 0, den, 1.0)


def _dot(a, b):
    return jnp.dot(a, b, preferred_element_type=F32)


def _dot_nt(a, b):
    return lax.dot_general(a, b, (((1,), (1,)), ((), ())), preferred_element_type=F32)


def _row_tile(m, pref):
    return pref if m % pref == 0 else m


def _rmsnorm_kernel(x_ref, g_ref, o_ref):
    o_ref[...] = _rms(x_ref[...], g_ref[...]).astype(o_ref.dtype)


def _rmsnorm(x, gain):
    m, d = x.shape
    tm = _row_tile(m, 512)
    return pl.pallas_call(
        _rmsnorm_kernel,
        grid=(m // tm,),
        in_specs=[pl.BlockSpec((tm, d), lambda i: (i, 0)), pl.BlockSpec((1, d), lambda i: (0, 0))],
        out_specs=pl.BlockSpec((tm, d), lambda i: (i, 0)),
        out_shape=jax.ShapeDtypeStruct((m, d), BF16),
        compiler_params=_cparams("parallel"),
        name="rmsnorm1",
    )(x, gain.reshape(1, d))


def _proj_q_kernel(h_ref, w_ref, qg_ref, q_ref, sbq_ref):
    acc = _dot(h_ref[...], w_ref[...])
    for hd in range(NSA_HEADS):
        sl = slice(hd * HEAD_DIM, (hd + 1) * HEAD_DIM)
        q_ref[:, sl] = _rms(acc[:, sl], qg_ref[...]).astype(BF16)
    sbq_ref[...] = acc[:, NSA_W:].astype(BF16)


def _proj_q(h, w, q_gain):
    m, d = h.shape
    tm = _row_tile(m, 512)
    n = NSA_W + SB_W
    return pl.pallas_call(
        _proj_q_kernel,
        grid=(m // tm,),
        in_specs=[pl.BlockSpec((tm, d), lambda i: (i, 0)), pl.BlockSpec((d, n), lambda i: (0, 0)),
                  pl.BlockSpec((1, HEAD_DIM), lambda i: (0, 0))],
        out_specs=[pl.BlockSpec((tm, NSA_W), lambda i: (i, 0)), pl.BlockSpec((tm, SB_W), lambda i: (i, 0))],
        out_shape=[jax.ShapeDtypeStruct((m, NSA_W), BF16), jax.ShapeDtypeStruct((m, SB_W), BF16)],
        compiler_params=_cparams("parallel"),
        name="proj_q",
    )(h, w, q_gain.reshape(1, HEAD_DIM))


def _proj_kv_kernel(h_ref, w_ref, kg_ref, cmp_ref, sel_ref, win_ref, gate_ref, cmph_ref):
    acc = _dot(h_ref[...], w_ref[...])
    row_w = 2 * KV_W
    cmp_ref[...] = acc[:, :row_w]
    for c in range(2 * NSA_KV):
        cmph_ref[c] = acc[:, c * HEAD_DIM:(c + 1) * HEAD_DIM].astype(BF16)
    for r, out_ref in ((1, sel_ref), (2, win_ref)):
        base = r * row_w
        for g in range(NSA_KV):
            sl = slice(base + g * HEAD_DIM, base + (g + 1) * HEAD_DIM)
            out_ref[:, g * HEAD_DIM:(g + 1) * HEAD_DIM] = _rms(acc[:, sl], kg_ref[r:r + 1, :])
        out_ref[:, KV_W:] = acc[:, base + KV_W:base + row_w]
    gate_ref[...] = jax.nn.sigmoid(acc[:, 3 * row_w:])


def _proj_kv(h, w, k_gain):
    m, d = h.shape
    tm = _row_tile(m, 512)
    n = 6 * KV_W + GATE_PAD
    row_w = 2 * KV_W
    row_spec = pl.BlockSpec((tm, row_w), lambda i: (i, 0))
    return pl.pallas_call(
        _proj_kv_kernel,
        grid=(m // tm,),
        in_specs=[pl.BlockSpec((tm, d), lambda i: (i, 0)), pl.BlockSpec((d, n), lambda i: (0, 0)),
                  pl.BlockSpec((3, HEAD_DIM), lambda i: (0, 0))],
        out_specs=[row_spec, row_spec, row_spec, pl.BlockSpec((tm, GATE_PAD), lambda i: (i, 0)),
                   pl.BlockSpec((2 * NSA_KV, tm, HEAD_DIM), lambda i: (0, i, 0))],
        out_shape=[jax.ShapeDtypeStruct((m, row_w), F32)] * 3
        + [jax.ShapeDtypeStruct((m, GATE_PAD), F32), jax.ShapeDtypeStruct((2 * NSA_KV, m, HEAD_DIM), BF16)],
        compiler_params=_cparams("parallel"),
        name="proj_kv",
    )(h, w, k_gain)


def _proj_sbkv_kernel(h_ref, w_ref, o_ref):
    o_ref[...] = _dot(h_ref[...], w_ref[...])


def _proj_sbkv(h, w):
    m, d = h.shape
    tm = _row_tile(m, 512)
    n = 2 * SB_W
    return pl.pallas_call(
        _proj_sbkv_kernel,
        grid=(m // tm,),
        in_specs=[pl.BlockSpec((tm, d), lambda i: (i, 0)), pl.BlockSpec((d, n), lambda i: (0, 0))],
        out_specs=pl.BlockSpec((tm, n), lambda i: (i, 0)),
        out_shape=jax.ShapeDtypeStruct((m, n), F32),
        compiler_params=_cparams("parallel"),
        name="proj_sbkv",
    )(h, w)


def _mix_kernel(h_ref, a_ref, b_ref, wga_ref, wgb_ref, wa_ref, wb_ref, o_ref):
    h = h_ref[...]
    ga = jax.nn.sigmoid(_dot(h, wga_ref[...]))
    gb = jax.nn.sigmoid(_dot(h, wgb_ref[...]))
    mixed = ga * _dot(a_ref[...], wa_ref[...]) + gb * _dot(b_ref[...], wb_ref[...])
    o_ref[...] = mixed.astype(o_ref.dtype)


def _mix(h, nsa_out, sb_out, w_ga, w_gb, w_up_nsa, w_up_sb):
    m, d = h.shape
    tm = _row_tile(m, 512)
    tn = 512
    return pl.pallas_call(
        _mix_kernel,
        grid=(d // tn, m // tm),
        in_specs=[pl.BlockSpec((tm, d), lambda j, i: (i, 0)),
                  pl.BlockSpec((tm, NSA_W), lambda j, i: (i, 0)),
                  pl.BlockSpec((tm, SB_W), lambda j, i: (i, 0)),
                  pl.BlockSpec((d, tn), lambda j, i: (0, j)),
                  pl.BlockSpec((d, tn), lambda j, i: (0, j)),
                  pl.BlockSpec((NSA_W, tn), lambda j, i: (0, j)),
                  pl.BlockSpec((SB_W, tn), lambda j, i: (0, j))],
        out_specs=pl.BlockSpec((tm, tn), lambda j, i: (i, j)),
        out_shape=jax.ShapeDtypeStruct((m, d), BF16),
        compiler_params=_cparams("parallel", "parallel"),
        name="mix",
    )(h, nsa_out, sb_out, w_ga, w_gb, w_up_nsa, w_up_sb)


def _out_proj_kernel(x_ref, mixed_ref, w_ref, g_ref, y_ref, hn_ref):
    y = x_ref[...] + _dot(mixed_ref[...], w_ref[...])
    y_ref[...] = y
    hn_ref[...] = _rms(y, g_ref[...]).astype(BF16)


def _out_proj(x, mixed, w_out, norm2_gain):
    m, d = x.shape
    tm = _row_tile(m, 512)
    row = pl.BlockSpec((tm, d), lambda i: (i, 0))
    return pl.pallas_call(
        _out_proj_kernel,
        grid=(m // tm,),
        in_specs=[row, row, pl.BlockSpec((d, d), lambda i: (0, 0)), pl.BlockSpec((1, d), lambda i: (0, 0))],
        out_specs=[row, row],
        out_shape=[jax.ShapeDtypeStruct((m, d), F32), jax.ShapeDtypeStruct((m, d), BF16)],
        compiler_params=_cparams("parallel"),
        name="out_proj",
    )(x, mixed, w_out, norm2_gain.reshape(1, d))


def _topk_rows(s, k, payloads=()):
    n = s.shape[0]
    iota = lax.broadcasted_iota(jnp.int32, s.shape, 0).astype(F32)
    vals, idxs = [], []
    picked = [[] for _ in payloads]
    for _ in range(k):
        m = jnp.max(s, axis=0, keepdims=True)
        idx = jnp.min(jnp.where(s == m, iota, float(n)), axis=0, keepdims=True)
        hit = iota == idx
        vals.append(m)
        idxs.append(idx)
        for p, acc in zip(payloads, picked):
            acc.append(jnp.sum(jnp.where(hit, p, 0.0), axis=0, keepdims=True))
        s = jnp.where(hit, NEG_INF, s)
    cat = lambda xs: jnp.concatenate(xs, axis=0)
    return cat(vals), cat(idxs), [cat(p) for p in picked]


_PEER_PAIRS = [(a, b) for a in range(PEER_TOPK) for b in range(PEER_TOPK) if (a + 1) * (b + 1) <= PEER_TOPK]


def _peer_route_kernel(hn_ref, wq_ref, keys_ref, g_ref, ia_ref, ib_ref):
    q = _dot(hn_ref[...], wq_ref[...]).astype(BF16)
    g_rows, a_rows, b_rows = [], [], []
    for hd in range(PEER_HEADS):
        qh = q[:, hd * PEER_DK:(hd + 1) * PEER_DK]
        s = _dot_nt(keys_ref[...], qh)
        s1, i1, _ = _topk_rows(s[:N_KEYS], PEER_TOPK)
        s2, i2, _ = _topk_rows(s[N_KEYS:], PEER_TOPK)
        cand = jnp.concatenate([s1[a:a + 1] + s2[b:b + 1] for a, b in _PEER_PAIRS], axis=0)
        pa = jnp.concatenate([i1[a:a + 1] for a, _ in _PEER_PAIRS], axis=0)
        pb = jnp.concatenate([i2[b:b + 1] for _, b in _PEER_PAIRS], axis=0)
        best, _, (ea, eb) = _topk_rows(cand, PEER_TOPK, (pa, pb))
        e = jnp.exp(best - best[0:1])
        g_rows.append(e / jnp.sum(e, axis=0, keepdims=True))
        a_rows.append(ea)
        b_rows.append(eb)
    g_ref[...] = jnp.concatenate(g_rows, axis=0).T
    ia_ref[...] = jnp.concatenate(a_rows, axis=0).T
    ib_ref[...] = jnp.concatenate(b_rows, axis=0).T


def _peer_route(hn, w_q, keys_bd):
    m, d = hn.shape
    tm = _row_tile(m, 256)
    n_slot = PEER_HEADS * PEER_TOPK
    slot = pl.BlockSpec((tm, n_slot), lambda i: (i, 0))
    return pl.pallas_call(
        _peer_route_kernel,
        grid=(m // tm,),
        in_specs=[pl.BlockSpec((tm, d), lambda i: (i, 0)),
                  pl.BlockSpec((d, PEER_HEADS * PEER_DK), lambda i: (0, 0)),
                  pl.BlockSpec((2 * N_KEYS, PEER_DK), lambda i: (0, 0))],
        out_specs=[slot, slot, slot],
        out_shape=[jax.ShapeDtypeStruct((m, n_slot), F32)] * 3,
        compiler_params=_cparams("parallel"),
        name="peer_route",
    )(hn, w_q, keys_bd)


EXPAND_GROUP = 16
EXPAND_PITCH = N_KEYS + 4


def _peer_expand_kernel(g_ref, ia_ref, ib_ref, w_ref, stage_ref):
    tm = g_ref.shape[0]
    row_id = lax.broadcasted_iota(jnp.int32, (N_KEYS, g_ref.shape[1]), 0).astype(F32)

    def group(gi, carry):
        t0 = pl.multiple_of(gi * EXPAND_GROUP, EXPAND_GROUP)
        for t in range(EXPAND_GROUP):
            g = g_ref[pl.ds(t0 + t, 1), :]
            rt = jnp.where(row_id == ia_ref[pl.ds(t0 + t, 1), :], g, 0.0).astype(BF16)
            ct = jnp.where(row_id == ib_ref[pl.ds(t0 + t, 1), :], 1.0, 0.0).astype(BF16)
            stage_ref[pl.ds(t * EXPAND_PITCH, N_KEYS), :] = _dot_nt(rt, ct)
        for a in range(N_KEYS):
            halves = [stage_ref[pl.ds(a + h * 8 * EXPAND_PITCH, 8, stride=EXPAND_PITCH), :]
                      for h in range(EXPAND_GROUP // 8)]
            w_ref[pl.ds(t0, EXPAND_GROUP), a * N_KEYS:(a + 1) * N_KEYS] = (
                jnp.concatenate(halves, axis=0).astype(w_ref.dtype))
        return carry

    lax.fori_loop(0, tm // EXPAND_GROUP, group, 0)


def _peer_expand(g, ia, ib):
    m, n_slot = g.shape
    tm = _row_tile(m, 128)
    slot = pl.BlockSpec((tm, n_slot), lambda i: (i, 0))
    return pl.pallas_call(
        _peer_expand_kernel,
        grid=(m // tm,),
        in_specs=[slot, slot, slot],
        out_specs=pl.BlockSpec((tm, N_EXPERTS), lambda i: (i, 0)),
        out_shape=jax.ShapeDtypeStruct((m, N_EXPERTS), BF16),
        scratch_shapes=[pltpu.VMEM((EXPAND_GROUP * EXPAND_PITCH, N_KEYS), F32)],
        compiler_params=_cparams("parallel"),
        name="peer_expand",
    )(g, ia, ib)


def _peer_dense_kernel(y_ref, hn_ref, w_ref, u_ref, v_ref, o_ref):
    @pl.when(pl.program_id(1) == 0)
    def _():
        o_ref[...] = y_ref[...]

    act = _dot_nt(hn_ref[...], u_ref[...])
    coeff = (w_ref[...].astype(F32) * _gelu(act)).astype(BF16)
    o_ref[...] += _dot(coeff, v_ref[...])


def _peer_dense(y, hn, w, u_tab, v_tab):
    m, d = y.shape
    tm = _row_tile(m, 1024)
    te = 512
    row = pl.BlockSpec((tm, d), lambda i, j: (i, 0))
    tab = pl.BlockSpec((te, d), lambda i, j: (j, 0))
    return pl.pallas_call(
        _peer_dense_kernel,
        grid=(m // tm, N_EXPERTS // te),
        in_specs=[row, row, pl.BlockSpec((tm, te), lambda i, j: (i, j)), tab, tab],
        out_specs=row,
        out_shape=jax.ShapeDtypeStruct((m, d), F32),
        compiler_params=_cparams("parallel", "arbitrary"),
        name="peer_dense",
    )(y, hn, w, u_tab, v_tab)


def _peer(y, hn, w_q, keys_bd, u_tab, v_tab):
    g, ia, ib = _peer_route(hn, w_q, keys_bd)
    return _peer_dense(y, hn, _peer_expand(g, ia, ib), u_tab, v_tab)


def _compress_kernel(ch_ref, w1a_ref, w1b_ref, pe_ref, w2_ref, kg_ref, o_ref):
    ch = ch_ref[...]
    n_chunk = ch.shape[0]
    half = w1a_ref.shape[0]
    pe = pe_ref[...]
    bias = _dot(pe[:, :half], w1a_ref[...]) + _dot(pe[:, half:], w1b_ref[...])
    nxt = pltpu.roll(_dot(ch, w1b_ref[...]), n_chunk - 1, 0)
    pre = _dot(ch, w1a_ref[...]) + nxt + bias[0:1]
    out = _dot(_gelu(pre).astype(BF16), w2_ref[...])
    is_key = pl.program_id(0) < NSA_KV
    o_ref[...] = jnp.where(is_key, _rms(out, kg_ref[...]), out)


def _compress(rows_hm, pe, w1, w2, k_gain, n_seq):
    n_c, total, d = rows_hm.shape
    n_chunk = total // n_seq // CMP_STRIDE
    half = CMP_STRIDE * d
    ch = rows_hm.reshape(n_c, n_seq, n_chunk, half)
    pe_flat = jnp.broadcast_to(pe.reshape(2, 1, 2 * half), (2, 8, 2 * half)).astype(BF16)
    w1 = w1.astype(BF16)
    kv = lambda c, b: (c // NSA_KV, 0, 0)
    return pl.pallas_call(
        _compress_kernel,
        grid=(n_c, n_seq),
        in_specs=[pl.BlockSpec((None, None, n_chunk, half), lambda c, b: (c, b, 0, 0)),
                  pl.BlockSpec((None, half, CMP_HID), kv),
                  pl.BlockSpec((None, half, CMP_HID), lambda c, b: (c // NSA_KV, 1, 0)),
                  pl.BlockSpec((None, 8, 2 * half), kv),
                  pl.BlockSpec((None, CMP_HID, d), kv),
                  pl.BlockSpec((1, d), lambda c, b: (0, 0))],
        out_specs=pl.BlockSpec((None, None, n_chunk, d), lambda c, b: (c, b, 0, 0)),
        out_shape=jax.ShapeDtypeStruct((n_c, n_seq, n_chunk, d), F32),
        compiler_params=_cparams("parallel", "parallel"),
        name="compress",
    )(ch, w1, w1, pe_flat, w2.astype(BF16), k_gain.reshape(1, d))


def _split3(x):
    a = x.astype(BF16)
    r = x - a.astype(F32)
    b = r.astype(BF16)
    c = (r - b.astype(F32)).astype(BF16)
    return a, b, c


def _nsa_prompt_kernel(q_ref, gate_ref, slope_ref, kc_ref, vc_ref, ks_ref, vs_ref, kw_ref, vw_ref, o_ref, os_ref,
                       *, tq):
    g = pl.program_id(1)
    t0 = pl.program_id(2) * tq
    seq = ks_ref.shape[0]
    n_cmp = kc_ref.shape[0]
    n_sel = seq // SEL_LEN
    rows = NSA_GROUP * tq
    q4 = jnp.concatenate([q_ref[:, j * HEAD_DIM:(j + 1) * HEAD_DIM] for j in range(NSA_GROUP)], axis=0)
    slope = slope_ref[...]
    tpos = t0 + lax.broadcasted_iota(jnp.int32, (tq, 1), 0)
    tpos4 = jnp.concatenate([tpos] * NSA_GROUP, axis=0)

    def attend(k, v, key_pos, mask):
        s = _dot_nt(q4, k.astype(BF16)) * ATT_SCALE - slope * (tpos4 - key_pos).astype(F32)
        p = _masked_softmax(s, mask)
        return p, _dot(p.astype(BF16), v.astype(BF16))

    c_end = lax.broadcasted_iota(jnp.int32, (1, n_cmp), 1) * CMP_STRIDE + (CMP_LEN - 1)
    p_c, o_c = attend(kc_ref[...], vc_ref[...], c_end, c_end <= tpos4)

    p_sum = p_c[0:tq]
    for j in range(1, NSA_GROUP):
        p_sum = p_sum + p_c[j * tq:(j + 1) * tq]
    ci = lax.broadcasted_iota(jnp.int32, (n_cmp, 128), 0)
    bj = lax.broadcasted_iota(jnp.int32, (n_cmp, 128), 1)
    ratio = SEL_LEN // CMP_STRIDE
    overlap = jnp.where((ci >= ratio * bj - 1) & (ci <= ratio * bj + ratio - 1) & (bj < n_sel), 1.0, 0.0).astype(BF16)
    imp = sum(_dot(part, overlap) for part in _split3(p_sum))

    blk = lax.broadcasted_iota(jnp.int32, (tq, 128), 1)
    cur = tpos // SEL_LEN
    forced = (blk == 0) | (blk == cur) | (blk == cur - 1)
    score = jnp.where(forced, FORCE_SCORE, jnp.where(blk * SEL_LEN <= tpos, imp, -1.0))
    score = jnp.where(blk < n_sel, score, -2.0)
    rank = jnp.zeros((tq, 128), F32)
    for i in range(n_sel):
        col = score[:, i:i + 1]
        rank = rank + jnp.where((col > score) | ((col == score) & (blk > i)), 1.0, 0.0)
    chosen = jnp.where((rank < float(min(TOP_N, n_sel))) & (blk < n_sel), 1.0, 0.0).astype(BF16)

    def sel_branch(n_keys):
        bi = lax.broadcasted_iota(jnp.int32, (128, n_keys), 0)
        kk = lax.broadcasted_iota(jnp.int32, (128, n_keys), 1)
        expand = jnp.where(kk // SEL_LEN == bi, 1.0, 0.0).astype(BF16)
        sel_keys = _dot(chosen, expand)
        sel4 = jnp.concatenate([sel_keys] * NSA_GROUP, axis=0)
        key_pos = lax.broadcasted_iota(jnp.int32, (1, n_keys), 1)
        _, o = attend(ks_ref[0:n_keys, :], vs_ref[0:n_keys, :], key_pos, (sel4 > 0.5) & (key_pos <= tpos4))
        os_ref[...] = o

    step = min(SEL_PREFIX_STEP, seq)
    for n_keys in range(step, seq + 1, step):
        pl.when((t0 + tq - 1) // step == n_keys // step - 1)(functools.partial(sel_branch, n_keys))
    o_s = os_ref[...]

    span = min(WINDOW + tq, seq)
    k0 = pl.multiple_of(jnp.clip(t0 - WINDOW, 0, seq - span), 8)
    win_pos = k0 + lax.broadcasted_iota(jnp.int32, (1, span), 1)
    dist = tpos4 - win_pos
    _, o_w = attend(kw_ref[pl.ds(k0, span), :], vw_ref[pl.ds(k0, span), :], win_pos, (dist >= 0) & (dist <= WINDOW))

    gates = jnp.where(g == 0, gate_ref[:, 0:3 * NSA_GROUP], gate_ref[:, 3 * NSA_GROUP:6 * NSA_GROUP])
    for j in range(NSA_GROUP):
        r = slice(j * tq, (j + 1) * tq)
        out = (gates[:, 3 * j:3 * j + 1] * o_c[r] + gates[:, 3 * j + 1:3 * j + 2] * o_s[r]
               + gates[:, 3 * j + 2:3 * j + 3] * o_w[r])
        o_ref[:, j * HEAD_DIM:(j + 1) * HEAD_DIM] = out.astype(o_ref.dtype)


def _alibi_rows(tq):
    h = np.arange(1, NSA_HEADS + 1, dtype=np.float32)
    s = np.exp2(-8.0 * h / NSA_HEADS).reshape(NSA_KV, NSA_GROUP)
    return jnp.asarray(np.repeat(s, tq, axis=1)[..., None])


def _nsa_prompt(q, gates, kcvc, sel_rows, win_rows, n_seq):
    total = q.shape[0]
    seq = total // n_seq
    tq = 64
    nq = seq // tq
    n_cmp = kcvc.shape[2]
    gw = NSA_GROUP * HEAD_DIM
    kv_k = pl.BlockSpec((seq, HEAD_DIM), lambda b, g, i: (b, g))
    kv_v = pl.BlockSpec((seq, HEAD_DIM), lambda b, g, i: (b, NSA_KV + g))
    return pl.pallas_call(
        functools.partial(_nsa_prompt_kernel, tq=tq),
        grid=(n_seq, NSA_KV, nq),
        in_specs=[pl.BlockSpec((tq, gw), lambda b, g, i: (b * nq + i, g)),
                  pl.BlockSpec((tq, GATE_PAD), lambda b, g, i: (b * nq + i, 0)),
                  pl.BlockSpec((None, NSA_GROUP * tq, 1), lambda b, g, i: (g, 0, 0)),
                  pl.BlockSpec((None, None, n_cmp, HEAD_DIM), lambda b, g, i: (g, b, 0, 0)),
                  pl.BlockSpec((None, None, n_cmp, HEAD_DIM), lambda b, g, i: (NSA_KV + g, b, 0, 0)),
                  kv_k, kv_v, kv_k, kv_v],
        out_specs=pl.BlockSpec((tq, gw), lambda b, g, i: (b * nq + i, g)),
        out_shape=jax.ShapeDtypeStruct((total, NSA_W), BF16),
        scratch_shapes=[pltpu.VMEM((NSA_GROUP * tq, HEAD_DIM), F32)],
        compiler_params=_cparams("parallel", "parallel", "arbitrary"),
        name="nsa_prompt",
    )(q, gates, _alibi_rows(tq), kcvc, kcvc, sel_rows, sel_rows, win_rows, win_rows)


def _after_matrix(n):
    si = lax.broadcasted_iota(jnp.int32, (n, n), 0)
    sj = lax.broadcasted_iota(jnp.int32, (n, n), 1)
    return jnp.where(si > sj, 1.0, 0.0).astype(BF16)


def _stick_block(z, carry, after, mask=None):
    log_keep_all = -(jnp.maximum(z, 0.0) + jnp.log(1.0 + jnp.exp(-jnp.abs(z))))
    log_keep = log_keep_all if mask is None else jnp.where(mask, log_keep_all, 0.0)
    hi = log_keep.astype(BF16)
    lo = (log_keep - hi.astype(F32)).astype(BF16)
    later = _dot(hi, after) + _dot(lo, after)
    a = jnp.exp(z + log_keep_all + later + carry)
    if mask is not None:
        a = jnp.where(mask, a, 0.0)
    return a, carry + jnp.sum(log_keep, axis=-1, keepdims=True)


def _sb_prompt_kernel(q_ref, k_ref, v_ref, o_ref, *, tq):
    i = pl.program_id(2)
    q = q_ref[...]
    n_blk = tq // SB_BLOCK
    t_pos = i * tq + lax.broadcasted_iota(jnp.int32, (tq, 1), 0)
    col = lax.broadcasted_iota(jnp.int32, (1, SB_BLOCK), 1)
    after = _after_matrix(SB_BLOCK)

    def chunk(j, carry, diagonal):
        k0 = pl.multiple_of(j * tq, tq)
        z = _dot_nt(q, k_ref[pl.ds(k0, tq), :].astype(BF16)) * ATT_SCALE
        parts = [None] * n_blk
        for s in reversed(range(n_blk)):
            mask = (k0 + s * SB_BLOCK + col) < t_pos if diagonal else None
            parts[s], carry = _stick_block(z[:, s * SB_BLOCK:(s + 1) * SB_BLOCK], carry, after, mask)
        a = jnp.concatenate(parts, axis=1).astype(BF16)
        return _dot(a, v_ref[pl.ds(k0, tq), :].astype(BF16)), carry

    acc, carry = chunk(i, jnp.zeros((tq, 1), F32), True)

    def cond(state):
        j, _, _, carry_max = state
        return (j >= 0) & (carry_max > EXP_UNDERFLOW)

    def body(state):
        j, carry, acc, _ = state
        o, carry = chunk(j, carry, False)
        return j - 1, carry, acc + o, jnp.max(carry)

    _, _, acc, _ = lax.while_loop(cond, body, (i - 1, carry, acc, jnp.max(carry)))
    o_ref[...] = acc.astype(o_ref.dtype)


def _sb_prompt(q, sb_rows, n_seq):
    total = q.shape[0]
    seq = total // n_seq
    tq = 256
    nq = seq // tq
    return pl.pallas_call(
        functools.partial(_sb_prompt_kernel, tq=tq),
        grid=(n_seq, SB_HEADS, nq),
        in_specs=[pl.BlockSpec((tq, HEAD_DIM), lambda b, h, i: (b * nq + i, h)),
                  pl.BlockSpec((seq, HEAD_DIM), lambda b, h, i: (b, h)),
                  pl.BlockSpec((seq, HEAD_DIM), lambda b, h, i: (b, SB_HEADS + h))],
        out_specs=pl.BlockSpec((tq, HEAD_DIM), lambda b, h, i: (b * nq + i, h)),
        out_shape=jax.ShapeDtypeStruct((total, SB_W), BF16),
        compiler_params=_cparams("parallel", "parallel", "arbitrary"),
        name="sb_prompt",
    )(q, sb_rows, sb_rows)


def _compress_paged_kernel(pt_ref, cache_ref, w1_ref, pe_ref, w2_ref, kg_ref, o_ref, buf, sem, nxt_ref, *, n_grp):
    n = pl.program_id(0)
    n_steps = pl.num_programs(0)
    pages = CMP_PAGES_PER_STEP
    page_rows = buf.shape[1] // pages
    chunk_rows = CMP_STRIDE * 2 * NSA_KV
    n_chunk = buf.shape[1] // chunk_rows
    half = CMP_STRIDE * HEAD_DIM

    def copies(step, slot):
        b = step // n_grp
        p0 = (n_grp - 1 - step % n_grp) * pages
        return [pltpu.make_async_copy(cache_ref.at[pt_ref[b, p0 + p]], buf.at[slot, pl.ds(p * page_rows, page_rows)],
                                      sem.at[slot]) for p in range(pages)]

    @pl.when(n == 0)
    def _():
        for c in copies(0, 0):
            c.start()

    @pl.when(n + 1 < n_steps)
    def _():
        for c in copies(n + 1, (n + 1) % 2):
            c.start()

    slot = n % 2
    for c in copies(n, slot):
        c.wait()

    @pl.when(n % n_grp == 0)
    def _():
        nxt_ref[...] = jnp.zeros_like(nxt_ref)

    last = lax.broadcasted_iota(jnp.int32, (n_chunk, 1), 0) == n_chunk - 1
    for kv in range(2):
        xs = []
        for g in range(NSA_KV):
            c = kv * NSA_KV + g
            pieces = [buf[slot, pl.ds(r * 2 * NSA_KV + c, n_chunk, stride=chunk_rows), :].astype(BF16)
                      for r in range(CMP_STRIDE)]
            xs.append(jnp.concatenate(pieces, axis=1))
        x = jnp.concatenate(xs, axis=0)
        w1a = w1_ref[kv, 0:half, :]
        w1b = w1_ref[kv, half:2 * half, :]
        pe = pe_ref[kv]
        bias = (_dot(pe[:, :half], w1a) + _dot(pe[:, half:], w1b))[0:1]
        left = _dot(x, w1a)
        right = _dot(x, w1b)
        for g in range(NSA_KV):
            c = kv * NSA_KV + g
            right_g = right[g * n_chunk:(g + 1) * n_chunk]
            nxt = jnp.where(last, nxt_ref[c, 0:1, :], pltpu.roll(right_g, n_chunk - 1, 0))
            pre = left[g * n_chunk:(g + 1) * n_chunk] + nxt + bias
            out = _dot(_gelu(pre).astype(BF16), w2_ref[kv])
            o_ref[c] = _rms(out, kg_ref[...]) if kv == 0 else out
            nxt_ref[c] = jnp.broadcast_to(right_g[0:1], nxt_ref.shape[1:])


def _compress_paged(cache, page_table, pe, w1, w2, k_gain):
    n_seq, n_pages = page_table.shape
    pages = CMP_PAGES_PER_STEP
    assert n_pages % pages == 0
    n_grp = n_pages // pages
    page_rows = cache.shape[1]
    d = cache.shape[2]
    half = CMP_STRIDE * d
    n_chunk = pages * PAGE_SIZE // CMP_STRIDE
    pe_flat = jnp.broadcast_to(pe.reshape(2, 1, 2 * half), (2, 8, 2 * half)).astype(BF16)
    const3 = lambda n, pt: (0, 0, 0)
    grid_spec = pltpu.PrefetchScalarGridSpec(
        num_scalar_prefetch=1,
        grid=(n_seq * n_grp,),
        in_specs=[pl.BlockSpec(memory_space=pl.ANY),
                  pl.BlockSpec((2, 2 * half, CMP_HID), const3),
                  pl.BlockSpec((2, 8, 2 * half), const3),
                  pl.BlockSpec((2, CMP_HID, d), const3),
                  pl.BlockSpec((1, d), lambda n, pt: (0, 0))],
        out_specs=pl.BlockSpec((2 * NSA_KV, None, n_chunk, d),
                               lambda n, pt: (0, n // n_grp, n_grp - 1 - n % n_grp, 0)),
        scratch_shapes=[pltpu.VMEM((2, pages * page_rows, d), F32), pltpu.SemaphoreType.DMA((2,)),
                        pltpu.VMEM((2 * NSA_KV, 8, CMP_HID), F32)])
    return pl.pallas_call(
        functools.partial(_compress_paged_kernel, n_grp=n_grp),
        grid_spec=grid_spec,
        out_shape=jax.ShapeDtypeStruct((2 * NSA_KV, n_seq, n_grp * n_chunk, d), F32),
        compiler_params=_cparams("arbitrary"),
        name="compress_paged",
    )(page_table, cache, w1.astype(BF16), pe_flat, w2.astype(BF16), k_gain.reshape(1, d))


HEAD_ROWS = 8


def _nsa_sample_kernel(pt_ref, q_ref, gate_ref, slope_ref, kc_ref, vc_ref, seln_ref, winn_ref, kw_ref, vw_ref,
                       cache_ref, o_ref, ksel, vsel, kband, vband, sem, *, past):
    b = pl.program_id(0)
    g = pl.program_id(1)
    q = q_ref[...]
    slope = slope_ref[...]
    n_cmp = kc_ref.shape[0]
    n_past_blk = past // SEL_LEN
    n_sel = n_past_blk + 1
    lanes = -(-n_sel // 128) * 128
    k_top = min(TOP_N, n_sel)
    blk_per_page = PAGE_SIZE // SEL_LEN

    def attend(k, v, dist, mask):
        s = _dot_nt(q, k.astype(BF16)) * ATT_SCALE - slope * dist.astype(F32)
        p = _masked_softmax(s, mask)
        return p, _dot(p.astype(BF16), v.astype(BF16))

    c_end = lax.broadcasted_iota(jnp.int32, (1, n_cmp), 1) * CMP_STRIDE + (CMP_LEN - 1)
    p_c, o_c = attend(kc_ref[...], vc_ref[...], past - c_end, c_end <= past)

    is_head = lax.broadcasted_iota(jnp.int32, (HEAD_ROWS, 1), 0) < NSA_GROUP
    p_sum = jnp.broadcast_to(jnp.sum(jnp.where(is_head, p_c, 0.0), axis=0, keepdims=True), (HEAD_ROWS, n_cmp))
    ci = lax.broadcasted_iota(jnp.int32, (n_cmp, lanes), 0)
    bj = lax.broadcasted_iota(jnp.int32, (n_cmp, lanes), 1)
    ratio = SEL_LEN // CMP_STRIDE
    overlap = jnp.where((ci >= ratio * bj - 1) & (ci <= ratio * bj + ratio - 1) & (bj < n_sel), 1.0, 0.0).astype(BF16)
    imp = sum(_dot(part, overlap) for part in _split3(p_sum))[0:1]
    blk = lax.broadcasted_iota(jnp.int32, (1, lanes), 1)
    blk_f = blk.astype(F32)
    cur = past // SEL_LEN
    forced = (blk == 0) | (blk == cur) | (blk == cur - 1)
    score = jnp.where(forced, FORCE_SCORE, jnp.where(blk * SEL_LEN <= past, imp, -1.0))
    score = jnp.where(blk < n_sel, score, -2.0)

    def sel_copies(idx, n):
        page = pt_ref[b, idx // blk_per_page]
        rows = pl.ds((idx % blk_per_page) * SEL_LEN, SEL_LEN)
        dst = pl.ds(n * SEL_LEN, SEL_LEN)
        k_col = pl.ds(pl.multiple_of(g * HEAD_DIM, HEAD_DIM), HEAD_DIM)
        v_col = pl.ds(pl.multiple_of((NSA_KV + g) * HEAD_DIM, HEAD_DIM), HEAD_DIM)
        return (pltpu.make_async_copy(cache_ref.at[page, rows, k_col], ksel.at[dst], sem.at[0]),
                pltpu.make_async_copy(cache_ref.at[page, rows, v_col], vsel.at[dst], sem.at[1]))

    first_row = lax.broadcasted_iota(jnp.int32, (SEL_LEN, 1), 0) == 0
    new_k = seln_ref[pl.ds(g, 1), :]
    new_v = seln_ref[pl.ds(NSA_KV + g, 1), :]
    picked = []
    for n in range(k_top):
        best = jnp.max(score)
        idx = jnp.min(jnp.where(score == best, blk_f, float(lanes))).astype(jnp.int32)
        score = jnp.where(blk == idx, NEG_INF, score)
        picked.append(idx)

        @pl.when(idx < n_past_blk)
        def _(idx=idx, n=n):
            for c in sel_copies(idx, n):
                c.start()

        @pl.when(idx >= n_past_blk)
        def _(n=n):
            ksel[n * SEL_LEN:(n + 1) * SEL_LEN, :] = jnp.where(first_row, new_k, 0.0)
            vsel[n * SEL_LEN:(n + 1) * SEL_LEN, :] = jnp.where(first_row, new_v, 0.0)

    wb = kw_ref.shape[0]
    pad_rows = kband.shape[0] - wb
    is_new_row = lax.broadcasted_iota(jnp.int32, (pad_rows, 1), 0) == 0
    kband[0:wb, :] = kw_ref[...]
    vband[0:wb, :] = vw_ref[...]
    kband[wb:wb + pad_rows, :] = jnp.where(is_new_row, winn_ref[pl.ds(g, 1), :], 0.0)
    vband[wb:wb + pad_rows, :] = jnp.where(is_new_row, winn_ref[pl.ds(NSA_KV + g, 1), :], 0.0)
    dist_w = wb - lax.broadcasted_iota(jnp.int32, (1, wb + pad_rows), 1)
    _, o_w = attend(kband[...], vband[...], dist_w, (dist_w >= 0) & (dist_w <= WINDOW))

    for n, idx in enumerate(picked):
        @pl.when(idx < n_past_blk)
        def _(idx=idx, n=n):
            for c in sel_copies(idx, n):
                c.wait()

    lane = lax.broadcasted_iota(jnp.int32, (1, k_top * SEL_LEN), 1)
    pos = jnp.zeros((1, k_top * SEL_LEN), jnp.int32)
    for n, idx in enumerate(picked):
        pos = jnp.where(lane // SEL_LEN == n, idx * SEL_LEN + lane - n * SEL_LEN, pos)
    dist_s = past - pos
    _, o_s = attend(ksel[...], vsel[...], dist_s, dist_s >= 0)

    gates = gate_ref[...]
    o_ref[...] = gates[:, 0:1] * o_c + gates[:, 1:2] * o_s + gates[:, 2:3] * o_w


def _nsa_sample(page_table, q, gates, kcvc, sel_new, win_new, win_state, cache_sel):
    n_seq, n_pages = page_table.shape
    past = n_pages * PAGE_SIZE
    n_cmp = kcvc.shape[2]
    wb = win_state.shape[1]
    d = HEAD_DIM
    k_top = min(TOP_N, past // SEL_LEN + 1)
    h = np.arange(1, NSA_HEADS + 1, dtype=np.float32)
    slopes = np.ones((NSA_KV, HEAD_ROWS, 1), np.float32)
    slopes[:, :NSA_GROUP, 0] = np.exp2(-8.0 * h / NSA_HEADS).reshape(NSA_KV, NSA_GROUP)
    head_blk = lambda: pl.BlockSpec((None, None, HEAD_ROWS, d), lambda b, g, pt: (b, g, 0, 0))
    new_blk = lambda: pl.BlockSpec((None, 2 * NSA_KV, d), lambda b, g, pt: (b, 0, 0))
    grid_spec = pltpu.PrefetchScalarGridSpec(
        num_scalar_prefetch=1,
        grid=(n_seq, NSA_KV),
        in_specs=[head_blk(), head_blk(),
                  pl.BlockSpec((None, HEAD_ROWS, 1), lambda b, g, pt: (g, 0, 0)),
                  pl.BlockSpec((None, None, n_cmp, d), lambda b, g, pt: (g, b, 0, 0)),
                  pl.BlockSpec((None, None, n_cmp, d), lambda b, g, pt: (NSA_KV + g, b, 0, 0)),
                  new_blk(), new_blk(),
                  pl.BlockSpec((None, wb, d), lambda b, g, pt: (b, 0, g)),
                  pl.BlockSpec((None, wb, d), lambda b, g, pt: (b, 0, NSA_KV + g)),
                  pl.BlockSpec(memory_space=pl.ANY)],
        out_specs=head_blk(),
        scratch_shapes=[pltpu.VMEM((k_top * SEL_LEN, d), F32), pltpu.VMEM((k_top * SEL_LEN, d), F32),
                        pltpu.VMEM((wb + 128, d), F32), pltpu.VMEM((wb + 128, d), F32),
                        pltpu.SemaphoreType.DMA((2,))])
    return pl.pallas_call(
        functools.partial(_nsa_sample_kernel, past=past),
        grid_spec=grid_spec,
        out_shape=jax.ShapeDtypeStruct((n_seq, NSA_KV, HEAD_ROWS, d), F32),
        compiler_params=_cparams("arbitrary", "arbitrary"),
        name="nsa_sample",
    )(page_table, q, gates, jnp.asarray(slopes), kcvc, kcvc, sel_new, win_new, win_state, win_state, cache_sel)


def _sb_sample_kernel(pt_ref, q_ref, cache_ref, o_ref, buf, sem):
    b = pl.program_id(0)
    n_pages = pt_ref.shape[1]
    q = q_ref[...]
    head = lax.broadcasted_iota(jnp.int32, (SB_HEADS, 1), 0)
    after = _after_matrix(PAGE_SIZE)
    kv_rows = 2 * SB_HEADS

    def page_copy(p, slot):
        return pltpu.make_async_copy(cache_ref.at[pt_ref[b, p]], buf.at[slot], sem.at[slot])

    def slot_of(p):
        return (n_pages - 1 - p) % 2

    page_copy(n_pages - 1, 0).start()

    def cond(state):
        p, _, _, carry_max = state
        return (p >= 0) & (carry_max > EXP_UNDERFLOW)

    def body(state):
        p, carry, acc, _ = state
        slot = slot_of(p)

        @pl.when(p > 0)
        def _():
            page_copy(p - 1, 1 - slot).start()

        page_copy(p, slot).wait()
        z = jnp.zeros((SB_HEADS, PAGE_SIZE), F32)
        for h in range(SB_HEADS):
            k_h = buf[slot, pl.ds(h, PAGE_SIZE, stride=kv_rows), :].astype(BF16)
            z = jnp.where(head == h, _dot_nt(q, k_h), z)
        a, carry = _stick_block(z * ATT_SCALE, carry, after)
        a = a.astype(BF16)
        for h in range(SB_HEADS):
            v_h = buf[slot, pl.ds(SB_HEADS + h, PAGE_SIZE, stride=kv_rows), :].astype(BF16)
            acc = acc + jnp.where(head == h, _dot(a, v_h), 0.0)
        return p - 1, carry, acc, jnp.max(carry)

    init = (n_pages - 1, jnp.zeros((SB_HEADS, 1), F32), jnp.zeros((SB_HEADS, HEAD_DIM), F32), jnp.float32(0.0))
    p, _, acc, _ = lax.while_loop(cond, body, init)

    @pl.when(p >= 0)
    def _():
        page_copy(p, slot_of(p)).wait()

    o_ref[...] = acc


def _sb_sample(page_table, q, cache):
    n_seq = page_table.shape[0]
    page_rows, d = cache.shape[1:]
    blk = lambda: pl.BlockSpec((None, SB_HEADS, d), lambda b, pt: (b, 0, 0))
    grid_spec = pltpu.PrefetchScalarGridSpec(
        num_scalar_prefetch=1,
        grid=(n_seq,),
        in_specs=[blk(), pl.BlockSpec(memory_space=pl.ANY)],
        out_specs=blk(),
        scratch_shapes=[pltpu.VMEM((2, page_rows, d), F32), pltpu.SemaphoreType.DMA((2,))])
    return pl.pallas_call(
        _sb_sample_kernel,
        grid_spec=grid_spec,
        out_shape=jax.ShapeDtypeStruct((n_seq, SB_HEADS, d), F32),
        compiler_params=_cparams("arbitrary"),
        name="sb_sample",
    )(page_table, q, cache)


def _split_w_in(w_in):
    w = w_in.astype(BF16)
    sizes = (NSA_W, KV_W, KV_W, KV_W, KV_W, KV_W, KV_W, GATE_W, SB_W, SB_W, SB_W, D_MODEL, D_MODEL)
    offs = np.concatenate([[0], np.cumsum(sizes)]).tolist()
    col = lambda a, b: w[:, offs[a]:offs[b]]
    w_q = jnp.concatenate([col(0, 1), col(8, 9)], axis=1)
    w_kv = jnp.concatenate([col(1, 7), jnp.pad(col(7, 8), ((0, 0), (0, GATE_PAD - GATE_W)))], axis=1)
    return w_q, w_kv, col(9, 11), col(11, 12), col(12, 13)


def _peer_keys(sub_keys):
    half = PEER_DK // 2
    z = jnp.zeros((N_KEYS, half), sub_keys.dtype)
    return jnp.concatenate([jnp.concatenate([sub_keys[0], z], axis=1),
                            jnp.concatenate([z, sub_keys[1]], axis=1)], axis=0).astype(BF16)


def kernel(x_prompt, x_sample, cache_cmp_kv, cache_sel_kv, cache_sb_kv, state_win_kv, page_table, norm1_gain, w_in,
           nsa_q_gain, nsa_k_gain, cmp_pe, cmp_w1, cmp_w2, w_up_nsa, w_up_sb, w_out, norm2_gain, peer_w_q,
           peer_sub_keys, peer_u, peer_v):
    assert w_in.shape[0] == 1, "single layer"
    B, T, D = x_prompt.shape
    S, TS, _ = x_sample.shape
    w_q, w_kv, w_sbkv, w_ga, w_gb = _split_w_in(w_in[0])
    w_un, w_us, w_o = w_up_nsa[0].astype(BF16), w_up_sb[0].astype(BF16), w_out[0].astype(BF16)
    wq_peer, keys_bd = peer_w_q[0].astype(BF16), _peer_keys(peer_sub_keys[0])
    u_tab, v_tab = peer_u[0].astype(BF16), peer_v[0].astype(BF16)
    k_gain = nsa_k_gain[0]

    def front(x2d):
        h = _rmsnorm(x2d, norm1_gain[0])
        q, sbq = _proj_q(h, w_q, nsa_q_gain[0])
        cmp_rows, sel_rows, win_rows, gates, cmp_hm = _proj_kv(h, w_kv, k_gain)
        sb_rows = _proj_sbkv(h, w_sbkv)
        return h, q, sbq, cmp_rows, sel_rows, win_rows, gates, cmp_hm, sb_rows

    def back(x2d, h, nsa_out, sb_out):
        mixed = _mix(h, nsa_out, sb_out, w_ga, w_gb, w_un, w_us)
        y, hn = _out_proj(x2d, mixed, w_o, norm2_gain[0])
        return _peer(y, hn, wq_peer, keys_bd, u_tab, v_tab)

    xp = x_prompt.reshape(B * T, D)
    h, q, sbq, cmp_rows, sel_rows, win_rows, gates, cmp_hm, sb_rows = front(xp)
    kcvc = _compress(cmp_hm, cmp_pe[0], cmp_w1[0], cmp_w2[0], k_gain[0], B)
    nsa_out = _nsa_prompt(q, gates, kcvc, sel_rows, win_rows, B)
    sb_out = _sb_prompt(sbq, sb_rows, B)
    y_p = back(xp, h, nsa_out, sb_out).reshape(B, T, D)
    kv5 = lambda rows, n, t, heads: rows.reshape(1, n, t, 2, heads, HEAD_DIM)
    cmp_p, sel_p, sb_p = kv5(cmp_rows, B, T, NSA_KV), kv5(sel_rows, B, T, NSA_KV), kv5(sb_rows, B, T, SB_HEADS)
    win_p = kv5(win_rows, B, T, NSA_KV)[:, :, -min(WINDOW, T):]

    assert TS == 1 and state_win_kv.shape[2] == WINDOW, "one new token per sequence after a past of >= WINDOW rows"
    n_phys = cache_cmp_kv.shape[1]
    xs = jnp.pad(x_sample.reshape(S, D), ((0, -S % SAMPLE_ROW_PAD), (0, 0)))
    h, *rest = front(xs)
    q, sbq, cmp_rows, sel_rows, win_rows, gates, cmp_hm, sb_rows = (a[..., :S, :] for a in rest)
    cmp_s, sel_s, sb_s = kv5(cmp_rows, S, TS, NSA_KV), kv5(sel_rows, S, TS, NSA_KV), kv5(sb_rows, S, TS, SB_HEADS)
    kcvc_s = _compress_paged(cache_cmp_kv[0].reshape(n_phys, PAGE_SIZE * 2 * NSA_KV, HEAD_DIM), page_table,
                             cmp_pe[0], cmp_w1[0], cmp_w2[0], k_gain[0])
    head_pad = ((0, 0), (0, 0), (0, HEAD_ROWS - NSA_GROUP), (0, 0))
    q_s = jnp.pad(q.reshape(S, NSA_KV, NSA_GROUP, HEAD_DIM), head_pad)
    gate_s = jnp.pad(gates[:, :GATE_W].reshape(S, NSA_KV, NSA_GROUP, 3), head_pad[:3] + ((0, 128 - 3),))
    nsa_s = _nsa_sample(page_table, q_s, gate_s, kcvc_s, sel_rows.reshape(S, 2 * NSA_KV, HEAD_DIM),
                        win_rows.reshape(S, 2 * NSA_KV, HEAD_DIM), state_win_kv[0].reshape(S, WINDOW, 2 * KV_W),
                        cache_sel_kv[0].reshape(n_phys, PAGE_SIZE, 2 * KV_W))
    sb_out_s = _sb_sample(page_table, sbq.reshape(S, SB_HEADS, HEAD_DIM),
                          cache_sb_kv[0].reshape(n_phys, PAGE_SIZE * 2 * SB_HEADS, HEAD_DIM))
    pad_rows = lambda a: jnp.pad(a, ((0, xs.shape[0] - S), (0, 0)))
    y_s = back(xs, h, pad_rows(nsa_s[:, :, :NSA_GROUP].reshape(S, NSA_W).astype(BF16)),
               pad_rows(sb_out_s.reshape(S, SB_W).astype(BF16)))
    y_s = y_s[:S].reshape(S, TS, D)
    new_win = jnp.concatenate([state_win_kv[:, :, 1:], kv5(win_rows, S, TS, NSA_KV)], axis=2)
    return (y_p, y_s, cmp_p, sel_p, sb_p, win_p, cmp_s, sel_s, sb_s, new_win)
```

```python
import functools

import numpy as np
import jax
import jax.numpy as jnp
from jax import lax
from jax.experimental import pallas as pl
from jax.experimental.pallas import tpu as pltpu

D_MODEL = 2048
PAGE_SIZE = 128
HEAD_DIM = 128
NSA_HEADS = 8
NSA_KV = 2
NSA_GROUP = NSA_HEADS // NSA_KV
NSA_W = NSA_HEADS * HEAD_DIM
KV_W = NSA_KV * HEAD_DIM
CMP_LEN = 32
CMP_STRIDE = 16
CMP_HID = 256
SEL_LEN = 64
TOP_N = 16
WINDOW = 512
FORCE_SCORE = 1.0e4
SB_HEADS = 8
SB_W = SB_HEADS * HEAD_DIM
PEER_HEADS = 8
PEER_TOPK = 16
N_KEYS = 128
N_EXPERTS = N_KEYS * N_KEYS
PEER_DK = 128
NORM_EPS = 1e-6
GATE_W = 3 * NSA_HEADS
GATE_PAD = 128
ATT_SCALE = HEAD_DIM ** -0.5
SAMPLE_ROW_PAD = 128
SEL_PREFIX_STEP = 512
SB_BLOCK = 128
SB_HEADS_PER_STEP = 2
EXP_UNDERFLOW = -104.0
MASK_BIAS = -1.0e30
DIST_SHIFT = 6
DIST_SPLIT = 1 << DIST_SHIFT
CMP_PAGES_PER_STEP = 32

F32 = jnp.float32
BF16 = jnp.bfloat16
NEG_INF = float("-inf")

VMEM_LIMIT_BYTES = 56 * 1024 * 1024


def _cparams(*sem):
    return pltpu.CompilerParams(dimension_semantics=sem, vmem_limit_bytes=VMEM_LIMIT_BYTES)


def _rms(x, gain):
    return x * lax.rsqrt(jnp.mean(x * x, axis=-1, keepdims=True) + NORM_EPS) * gain


def _gelu(x):
    return 0.5 * x * (1.0 + jnp.tanh(0.7978845608028654 * (x + 0.044715 * (x * x * x))))


def _masked_softmax(s, mask):
    sm = jnp.where(mask, s, NEG_INF)
    m = jnp.max(sm, axis=-1, keepdims=True)
    m = jnp.where(m > NEG_INF, m, 0.0)
    e = jnp.where(mask, jnp.exp(s - m), 0.0)
    den = jnp.sum(e, axis=-1, keepdims=True)
    return e / jnp.where(den > 0, den, 1.0)


def _dot(a, b):
    return jnp.dot(a, b, preferred_element_type=F32)


def _dot_nt(a, b):
    return lax.dot_general(a, b, (((1,), (1,)), ((), ())), preferred_element_type=F32)


def _row_tile(m, pref):
    return pref if m % pref == 0 else m


def _rmsnorm_kernel(x_ref, g_ref, o_ref):
    o_ref[...] = _rms(x_ref[...], g_ref[...]).astype(o_ref.dtype)


def _rmsnorm(x, gain):
    m, d = x.shape
    tm = _row_tile(m, 512)
    return pl.pallas_call(
        _rmsnorm_kernel,
        grid=(m // tm,),
        in_specs=[pl.BlockSpec((tm, d), lambda i: (i, 0)), pl.BlockSpec((1, d), lambda i: (0, 0))],
        out_specs=pl.BlockSpec((tm, d), lambda i: (i, 0)),
        out_shape=jax.ShapeDtypeStruct((m, d), BF16),
        compiler_params=_cparams("parallel"),
        name="rmsnorm1",
    )(x, gain.reshape(1, d))


def _proj_q_kernel(h_ref, w_ref, qg_ref, q_ref, sbq_ref):
    acc = _dot(h_ref[...], w_ref[...])
    for hd in range(NSA_HEADS):
        sl = slice(hd * HEAD_DIM, (hd + 1) * HEAD_DIM)
        q_ref[:, sl] = (_rms(acc[:, sl], qg_ref[...]) * ATT_SCALE).astype(BF16)
    sbq_ref[...] = (acc[:, NSA_W:] * ATT_SCALE).astype(BF16)


def _proj_q(h, w, q_gain):
    m, d = h.shape
    tm = _row_tile(m, 512)
    n = NSA_W + SB_W
    return pl.pallas_call(
        _proj_q_kernel,
        grid=(m // tm,),
        in_specs=[pl.BlockSpec((tm, d), lambda i: (i, 0)), pl.BlockSpec((d, n), lambda i: (0, 0)),
                  pl.BlockSpec((1, HEAD_DIM), lambda i: (0, 0))],
        out_specs=[pl.BlockSpec((tm, NSA_W), lambda i: (i, 0)), pl.BlockSpec((tm, SB_W), lambda i: (i, 0))],
        out_shape=[jax.ShapeDtypeStruct((m, NSA_W), BF16), jax.ShapeDtypeStruct((m, SB_W), BF16)],
        compiler_params=_cparams("parallel"),
        name="proj_q",
    )(h, w, q_gain.reshape(1, HEAD_DIM))


def _proj_kv_kernel(h_ref, w_ref, kg_ref, cmp_ref, sel_ref, win_ref, gate_ref, cmph_ref, kvb_ref):
    acc = _dot(h_ref[...], w_ref[...])
    row_w = 2 * KV_W
    cmp_ref[...] = acc[:, :row_w]
    for c in range(2 * NSA_KV):
        cmph_ref[c] = acc[:, c * HEAD_DIM:(c + 1) * HEAD_DIM].astype(BF16)
    for r, out_ref in ((1, sel_ref), (2, win_ref)):
        base = r * row_w
        for g in range(NSA_KV):
            sl = slice(base + g * HEAD_DIM, base + (g + 1) * HEAD_DIM)
            out_ref[:, g * HEAD_DIM:(g + 1) * HEAD_DIM] = _rms(acc[:, sl], kg_ref[r:r + 1, :])
        out_ref[:, KV_W:] = acc[:, base + KV_W:base + row_w]
        kvb_ref[:, (r - 1) * row_w:r * row_w] = out_ref[...].astype(BF16)
    gate_ref[...] = jax.nn.sigmoid(acc[:, 3 * row_w:])


def _proj_kv(h, w, k_gain):
    m, d = h.shape
    tm = _row_tile(m, 512)
    n = 6 * KV_W + GATE_PAD
    row_w = 2 * KV_W
    row_spec = pl.BlockSpec((tm, row_w), lambda i: (i, 0))
    return pl.pallas_call(
        _proj_kv_kernel,
        grid=(m // tm,),
        in_specs=[pl.BlockSpec((tm, d), lambda i: (i, 0)), pl.BlockSpec((d, n), lambda i: (0, 0)),
                  pl.BlockSpec((3, HEAD_DIM), lambda i: (0, 0))],
        out_specs=[row_spec, row_spec, row_spec, pl.BlockSpec((tm, GATE_PAD), lambda i: (i, 0)),
                   pl.BlockSpec((2 * NSA_KV, tm, HEAD_DIM), lambda i: (0, i, 0)),
                   pl.BlockSpec((tm, 2 * row_w), lambda i: (i, 0))],
        out_shape=[jax.ShapeDtypeStruct((m, row_w), F32)] * 3
        + [jax.ShapeDtypeStruct((m, GATE_PAD), F32), jax.ShapeDtypeStruct((2 * NSA_KV, m, HEAD_DIM), BF16),
           jax.ShapeDtypeStruct((m, 2 * row_w), BF16)],
        compiler_params=_cparams("parallel"),
        name="proj_kv",
    )(h, w, k_gain)


def _proj_sbkv_kernel(h_ref, w_ref, o_ref):
    o_ref[...] = _dot(h_ref[...], w_ref[...])


def _proj_sbkv(h, w):
    m, d = h.shape
    tm = _row_tile(m, 512)
    n = 2 * SB_W
    return pl.pallas_call(
        _proj_sbkv_kernel,
        grid=(m // tm,),
        in_specs=[pl.BlockSpec((tm, d), lambda i: (i, 0)), pl.BlockSpec((d, n), lambda i: (0, 0))],
        out_specs=pl.BlockSpec((tm, n), lambda i: (i, 0)),
        out_shape=jax.ShapeDtypeStruct((m, n), F32),
        compiler_params=_cparams("parallel"),
        name="proj_sbkv",
    )(h, w)


def _mix_kernel(h_ref, a_ref, b_ref, wga_ref, wgb_ref, wa_ref, wb_ref, o_ref):
    h = h_ref[...]
    ga = jax.nn.sigmoid(_dot(h, wga_ref[...]))
    gb = jax.nn.sigmoid(_dot(h, wgb_ref[...]))
    mixed = ga * _dot(a_ref[...], wa_ref[...]) + gb * _dot(b_ref[...], wb_ref[...])
    o_ref[...] = mixed.astype(o_ref.dtype)


def _mix(h, nsa_out, sb_out, w_ga, w_gb, w_up_nsa, w_up_sb):
    m, d = h.shape
    tm = _row_tile(m, 512)
    tn = 512
    return pl.pallas_call(
        _mix_kernel,
        grid=(d // tn, m // tm),
        in_specs=[pl.BlockSpec((tm, d), lambda j, i: (i, 0)),
                  pl.BlockSpec((tm, NSA_W), lambda j, i: (i, 0)),
                  pl.BlockSpec((tm, SB_W), lambda j, i: (i, 0)),
                  pl.BlockSpec((d, tn), lambda j, i: (0, j)),
                  pl.BlockSpec((d, tn), lambda j, i: (0, j)),
                  pl.BlockSpec((NSA_W, tn), lambda j, i: (0, j)),
                  pl.BlockSpec((SB_W, tn), lambda j, i: (0, j))],
        out_specs=pl.BlockSpec((tm, tn), lambda j, i: (i, j)),
        out_shape=jax.ShapeDtypeStruct((m, d), BF16),
        compiler_params=_cparams("parallel", "parallel"),
        name="mix",
    )(h, nsa_out, sb_out, w_ga, w_gb, w_up_nsa, w_up_sb)


def _out_proj_kernel(x_ref, mixed_ref, w_ref, g_ref, y_ref, hn_ref):
    y = x_ref[...] + _dot(mixed_ref[...], w_ref[...])
    y_ref[...] = y
    hn_ref[...] = _rms(y, g_ref[...]).astype(BF16)


def _out_proj(x, mixed, w_out, norm2_gain):
    m, d = x.shape
    tm = _row_tile(m, 512)
    row = pl.BlockSpec((tm, d), lambda i: (i, 0))
    return pl.pallas_call(
        _out_proj_kernel,
        grid=(m // tm,),
        in_specs=[row, row, pl.BlockSpec((d, d), lambda i: (0, 0)), pl.BlockSpec((1, d), lambda i: (0, 0))],
        out_specs=[row, row],
        out_shape=[jax.ShapeDtypeStruct((m, d), F32), jax.ShapeDtypeStruct((m, d), BF16)],
        compiler_params=_cparams("parallel"),
        name="out_proj",
    )(x, mixed, w_out, norm2_gain.reshape(1, d))


def _topk_rows(s, k, payloads=()):
    n = s.shape[0]
    iota = lax.broadcasted_iota(jnp.int32, s.shape, 0).astype(F32)
    vals, idxs = [], []
    picked = [[] for _ in payloads]
    for _ in range(k):
        m = jnp.max(s, axis=0, keepdims=True)
        idx = jnp.min(jnp.where(s == m, iota, float(n)), axis=0, keepdims=True)
        hit = iota == idx
        vals.append(m)
        idxs.append(idx)
        for p, acc in zip(payloads, picked):
            acc.append(jnp.sum(jnp.where(hit, p, 0.0), axis=0, keepdims=True))
        s = jnp.where(hit, NEG_INF, s)
    cat = lambda xs: jnp.concatenate(xs, axis=0)
    return cat(vals), cat(idxs), [cat(p) for p in picked]


_PEER_PAIRS = [(a, b) for a in range(PEER_TOPK) for b in range(PEER_TOPK) if (a + 1) * (b + 1) <= PEER_TOPK]


def _peer_route_kernel(hn_ref, wq_ref, keys_ref, g_ref, ia_ref, ib_ref):
    q = _dot(hn_ref[...], wq_ref[...]).astype(BF16)
    g_rows, a_rows, b_rows = [], [], []
    for hd in range(PEER_HEADS):
        qh = q[:, hd * PEER_DK:(hd + 1) * PEER_DK]
        s = _dot_nt(keys_ref[...], qh)
        s1, i1, _ = _topk_rows(s[:N_KEYS], PEER_TOPK)
        s2, i2, _ = _topk_rows(s[N_KEYS:], PEER_TOPK)
        cand = jnp.concatenate([s1[a:a + 1] + s2[b:b + 1] for a, b in _PEER_PAIRS], axis=0)
        pa = jnp.concatenate([i1[a:a + 1] for a, _ in _PEER_PAIRS], axis=0)
        pb = jnp.concatenate([i2[b:b + 1] for _, b in _PEER_PAIRS], axis=0)
        best, _, (ea, eb) = _topk_rows(cand, PEER_TOPK, (pa, pb))
        e = jnp.exp(best - best[0:1])
        g_rows.append(e / jnp.sum(e, axis=0, keepdims=True))
        a_rows.append(ea)
        b_rows.append(eb)
    g_ref[...] = jnp.concatenate(g_rows, axis=0).T
    ia_ref[...] = jnp.concatenate(a_rows, axis=0).T
    ib_ref[...] = jnp.concatenate(b_rows, axis=0).T


def _peer_route(hn, w_q, keys_bd):
    m, d = hn.shape
    tm = _row_tile(m, 256)
    n_slot = PEER_HEADS * PEER_TOPK
    slot = pl.BlockSpec((tm, n_slot), lambda i: (i, 0))
    return pl.pallas_call(
        _peer_route_kernel,
        grid=(m // tm,),
        in_specs=[pl.BlockSpec((tm, d), lambda i: (i, 0)),
                  pl.BlockSpec((d, PEER_HEADS * PEER_DK), lambda i: (0, 0)),
                  pl.BlockSpec((2 * N_KEYS, PEER_DK), lambda i: (0, 0))],
        out_specs=[slot, slot, slot],
        out_shape=[jax.ShapeDtypeStruct((m, n_slot), F32)] * 3,
        compiler_params=_cparams("parallel"),
        name="peer_route",
    )(hn, w_q, keys_bd)


EXPAND_GROUP = 16
EXPAND_PITCH = N_KEYS + 4


def _peer_expand_kernel(g_ref, ia_ref, ib_ref, w_ref, stage_ref):
    tm = g_ref.shape[0]
    row_id = lax.broadcasted_iota(jnp.int32, (N_KEYS, g_ref.shape[1]), 0).astype(F32)

    def group(gi, carry):
        t0 = pl.multiple_of(gi * EXPAND_GROUP, EXPAND_GROUP)
        for t in range(EXPAND_GROUP):
            g = g_ref[pl.ds(t0 + t, 1), :]
            rt = jnp.where(row_id == ia_ref[pl.ds(t0 + t, 1), :], g, 0.0).astype(BF16)
            ct = jnp.where(row_id == ib_ref[pl.ds(t0 + t, 1), :], 1.0, 0.0).astype(BF16)
            stage_ref[pl.ds(t * EXPAND_PITCH, N_KEYS), :] = _dot_nt(rt, ct)
        for a in range(N_KEYS):
            halves = [stage_ref[pl.ds(a + h * 8 * EXPAND_PITCH, 8, stride=EXPAND_PITCH), :]
                      for h in range(EXPAND_GROUP // 8)]
            w_ref[pl.ds(t0, EXPAND_GROUP), a * N_KEYS:(a + 1) * N_KEYS] = (
                jnp.concatenate(halves, axis=0).astype(w_ref.dtype))
        return carry

    lax.fori_loop(0, tm // EXPAND_GROUP, group, 0)


def _peer_expand(g, ia, ib):
    m, n_slot = g.shape
    tm = _row_tile(m, 128)
    slot = pl.BlockSpec((tm, n_slot), lambda i: (i, 0))
    return pl.pallas_call(
        _peer_expand_kernel,
        grid=(m // tm,),
        in_specs=[slot, slot, slot],
        out_specs=pl.BlockSpec((tm, N_EXPERTS), lambda i: (i, 0)),
        out_shape=jax.ShapeDtypeStruct((m, N_EXPERTS), BF16),
        scratch_shapes=[pltpu.VMEM((EXPAND_GROUP * EXPAND_PITCH, N_KEYS), F32)],
        compiler_params=_cparams("parallel"),
        name="peer_expand",
    )(g, ia, ib)


def _peer_dense_kernel(y_ref, hn_ref, w_ref, u_ref, v_ref, o_ref):
    @pl.when(pl.program_id(1) == 0)
    def _():
        o_ref[...] = y_ref[...]

    act = _dot_nt(hn_ref[...], u_ref[...])
    coeff = (w_ref[...].astype(F32) * _gelu(act)).astype(BF16)
    o_ref[...] += _dot(coeff, v_ref[...])


def _peer_dense(y, hn, w, u_tab, v_tab):
    m, d = y.shape
    tm = _row_tile(m, 1024)
    te = 512
    row = pl.BlockSpec((tm, d), lambda i, j: (i, 0))
    tab = pl.BlockSpec((te, d), lambda i, j: (j, 0))
    return pl.pallas_call(
        _peer_dense_kernel,
        grid=(m // tm, N_EXPERTS // te),
        in_specs=[row, row, pl.BlockSpec((tm, te), lambda i, j: (i, j)), tab, tab],
        out_specs=row,
        out_shape=jax.ShapeDtypeStruct((m, d), F32),
        compiler_params=_cparams("parallel", "arbitrary"),
        name="peer_dense",
    )(y, hn, w, u_tab, v_tab)


def _peer(y, hn, w_q, keys_bd, u_tab, v_tab):
    g, ia, ib = _peer_route(hn, w_q, keys_bd)
    return _peer_dense(y, hn, _peer_expand(g, ia, ib), u_tab, v_tab)


def _compress_kernel(ch_ref, w1a_ref, w1b_ref, pe_ref, w2_ref, kg_ref, o_ref):
    ch = ch_ref[...]
    n_chunk = ch.shape[0]
    half = w1a_ref.shape[0]
    pe = pe_ref[...]
    bias = _dot(pe[:, :half], w1a_ref[...]) + _dot(pe[:, half:], w1b_ref[...])
    nxt = pltpu.roll(_dot(ch, w1b_ref[...]), n_chunk - 1, 0)
    pre = _dot(ch, w1a_ref[...]) + nxt + bias[0:1]
    out = _dot(_gelu(pre).astype(BF16), w2_ref[...])
    is_key = pl.program_id(0) < NSA_KV
    o_ref[...] = jnp.where(is_key, _rms(out, kg_ref[...]), out)


def _compress(rows_hm, pe, w1, w2, k_gain, n_seq):
    n_c, total, d = rows_hm.shape
    n_chunk = total // n_seq // CMP_STRIDE
    half = CMP_STRIDE * d
    ch = rows_hm.reshape(n_c, n_seq, n_chunk, half)
    pe_flat = jnp.broadcast_to(pe.reshape(2, 1, 2 * half), (2, 8, 2 * half)).astype(BF16)
    w1 = w1.astype(BF16)
    kv = lambda c, b: (c // NSA_KV, 0, 0)
    return pl.pallas_call(
        _compress_kernel,
        grid=(n_c, n_seq),
        in_specs=[pl.BlockSpec((None, None, n_chunk, half), lambda c, b: (c, b, 0, 0)),
                  pl.BlockSpec((None, half, CMP_HID), kv),
                  pl.BlockSpec((None, half, CMP_HID), lambda c, b: (c // NSA_KV, 1, 0)),
                  pl.BlockSpec((None, 8, 2 * half), kv),
                  pl.BlockSpec((None, CMP_HID, d), kv),
                  pl.BlockSpec((1, d), lambda c, b: (0, 0))],
        out_specs=pl.BlockSpec((None, None, n_chunk, d), lambda c, b: (c, b, 0, 0)),
        out_shape=jax.ShapeDtypeStruct((n_c, n_seq, n_chunk, d), F32),
        compiler_params=_cparams("parallel", "parallel"),
        name="compress",
    )(ch, w1, w1, pe_flat, w2.astype(BF16), k_gain.reshape(1, d))


def _split3(x):
    a = x.astype(BF16)
    r = x - a.astype(F32)
    b = r.astype(BF16)
    c = (r - b.astype(F32)).astype(BF16)
    return a, b, c


def _nsa_prompt_kernel(q_ref, gate_ref, slope_ref, coef_ref, tab_s_ref, tab_w_ref, kc_ref, vc_ref, ks_ref, vs_ref,
                       kw_ref, vw_ref, o_ref, os_ref, *, tq):
    g = pl.program_id(1)
    cur = pl.program_id(2)
    t0 = cur * tq
    seq = ks_ref.shape[0]
    n_cmp = kc_ref.shape[0]
    n_sel = seq // SEL_LEN
    rel0 = seq - tq
    q4 = jnp.concatenate([q_ref[:, j * HEAD_DIM:(j + 1) * HEAD_DIM] for j in range(NSA_GROUP)], axis=0)
    slope = slope_ref[...]
    tpos = t0 + lax.broadcasted_iota(jnp.int32, (tq, 1), 0)
    tpos4 = jnp.concatenate([tpos] * NSA_GROUP, axis=0)
    coef = coef_ref[...]

    def attend_biased(lhs, k, bias_keys, v):
        s = _dot_nt(lhs, jnp.concatenate([k, bias_keys], axis=1))
        e = jnp.exp(s - jnp.max(s, axis=-1, keepdims=True))
        den = jnp.sum(e, axis=-1, keepdims=True)
        return _dot(e.astype(BF16), v) * (1.0 / den)

    c_end = lax.broadcasted_iota(jnp.int32, (1, n_cmp), 1) * CMP_STRIDE + (CMP_LEN - 1)
    s_c = _dot_nt(q4, kc_ref[...].astype(BF16)) - slope * (tpos4 - c_end).astype(F32)
    p_c = _masked_softmax(s_c, c_end <= tpos4)
    o_c = _dot(p_c.astype(BF16), vc_ref[...].astype(BF16))

    p_sum = p_c[0:tq]
    for j in range(1, NSA_GROUP):
        p_sum = p_sum + p_c[j * tq:(j + 1) * tq]
    ci = lax.broadcasted_iota(jnp.int32, (n_cmp, 128), 0)
    bj = lax.broadcasted_iota(jnp.int32, (n_cmp, 128), 1)
    ratio = SEL_LEN // CMP_STRIDE
    overlap = jnp.where((ci >= ratio * bj - 1) & (ci <= ratio * bj + ratio - 1) & (bj < n_sel), 1.0, 0.0).astype(BF16)
    imp = sum(_dot(part, overlap) for part in _split3(p_sum))

    blk = lax.broadcasted_iota(jnp.int32, (tq, 128), 1)
    forced = (blk == 0) | (blk == cur) | (blk == cur - 1)
    score = jnp.where(forced, FORCE_SCORE, jnp.where(blk * SEL_LEN <= tpos, imp, -1.0))
    score = jnp.where(blk < n_sel, score, -2.0)
    rank = jnp.zeros((tq, 128), F32)
    for i in range(n_sel):
        col = score[:, i:i + 1]
        rank = rank + jnp.where((col > score) | ((col == score) & (blk > i)), 1.0, 0.0)
    chosen = (rank < float(min(TOP_N, n_sel))) & (blk <= cur)
    not_chosen = jnp.where(chosen, 0.0, MASK_BIAS).astype(BF16)
    fi = lax.broadcasted_iota(jnp.int32, (128, 128), 0)
    fj = lax.broadcasted_iota(jnp.int32, (128, 128), 1)
    flip = jnp.where(fi + fj == cur, 1.0, 0.0).astype(BF16)
    not_chosen_rel = _dot(not_chosen, flip).astype(BF16)
    lane = lax.broadcasted_iota(jnp.int32, coef.shape, 1)
    coef_sel = jnp.where(lane < n_sel, jnp.concatenate([not_chosen_rel] * NSA_GROUP, axis=0), coef)
    tab_row = pl.multiple_of(rel0 - t0, 16)

    def sel_branch(n_keys):
        os_ref[...] = attend_biased(jnp.concatenate([q4, coef_sel], axis=1), ks_ref[0:n_keys, :],
                                    tab_s_ref[pl.ds(tab_row, n_keys), :], vs_ref[0:n_keys, :])

    step = min(SEL_PREFIX_STEP, seq)
    for n_keys in range(step, seq + 1, step):
        pl.when((t0 + tq - 1) // step == n_keys // step - 1)(functools.partial(sel_branch, n_keys))
    o_s = os_ref[...]

    span = min(WINDOW + tq, seq)
    k0 = pl.multiple_of(jnp.clip(t0 - WINDOW, 0, seq - span), 16)
    o_w = attend_biased(jnp.concatenate([q4, coef], axis=1), kw_ref[pl.ds(k0, span), :],
                        tab_w_ref[pl.ds(pl.multiple_of(tab_row + k0, 16), span), :], vw_ref[pl.ds(k0, span), :])

    gates = jnp.where(g == 0, gate_ref[:, 0:3 * NSA_GROUP], gate_ref[:, 3 * NSA_GROUP:6 * NSA_GROUP])
    for j in range(NSA_GROUP):
        r = slice(j * tq, (j + 1) * tq)
        out = (gates[:, 3 * j:3 * j + 1] * o_c[r] + gates[:, 3 * j + 1:3 * j + 2] * o_s[r]
               + gates[:, 3 * j + 2:3 * j + 3] * o_w[r])
        o_ref[:, j * HEAD_DIM:(j + 1) * HEAD_DIM] = out.astype(o_ref.dtype)


def _alibi_slopes():
    h = np.arange(1, NSA_HEADS + 1, dtype=np.float32)
    return np.exp2(-8.0 * h / NSA_HEADS).reshape(NSA_KV, NSA_GROUP)


def _nsa_prompt_consts(tq, seq):
    n_sel = seq // SEL_LEN
    assert tq == SEL_LEN and n_sel + tq + 2 <= 128
    slopes = np.repeat(_alibi_slopes(), tq, axis=1)
    rows = slopes.shape[1]
    coef = np.zeros((NSA_KV, rows, 128), np.float32)
    coef[:, np.arange(rows), n_sel + np.arange(rows) % tq] = 1.0
    coef[:, :, n_sel + tq] = -slopes * DIST_SPLIT
    coef[:, :, n_sel + tq + 1] = -slopes
    rel = np.arange(-(-(2 * seq - tq) // 16) * 16)[:, None] - (seq - tq)
    r = np.arange(tq)[None, :]
    dist = tq - rel[:, 0]

    def table(window):
        tab = np.zeros((rel.shape[0], 128), np.float32)
        tab[:, :n_sel] = (rel < tq) & (-(rel // SEL_LEN) == np.arange(n_sel)[None, :])
        unreadable = (rel > r) | ((r - rel > WINDOW) if window else False)
        tab[:, n_sel:n_sel + tq] = np.where(unreadable, MASK_BIAS, 0.0)
        tab[:, n_sel + tq] = dist >> DIST_SHIFT
        tab[:, n_sel + tq + 1] = dist & (DIST_SPLIT - 1)
        return jnp.asarray(tab, dtype=BF16)

    return jnp.asarray(slopes[..., None]), jnp.asarray(coef, dtype=BF16), table(False), table(True)


def _nsa_prompt(q, gates, kcvc, kv_bf, n_seq):
    total = q.shape[0]
    seq = total // n_seq
    tq = SEL_LEN
    nq = seq // tq
    n_cmp = kcvc.shape[2]
    gw = NSA_GROUP * HEAD_DIM
    rows = NSA_GROUP * tq
    slopes, coef, tab_s, tab_w = _nsa_prompt_consts(tq, seq)
    kv_col = lambda c: pl.BlockSpec((seq, HEAD_DIM), lambda b, g, i: (b, c * NSA_KV + g))
    table = pl.BlockSpec(tab_s.shape, lambda b, g, i: (0, 0))
    return pl.pallas_call(
        functools.partial(_nsa_prompt_kernel, tq=tq),
        grid=(n_seq, NSA_KV, nq),
        in_specs=[pl.BlockSpec((tq, gw), lambda b, g, i: (b * nq + i, g)),
                  pl.BlockSpec((tq, GATE_PAD), lambda b, g, i: (b * nq + i, 0)),
                  pl.BlockSpec((None, rows, 1), lambda b, g, i: (g, 0, 0)),
                  pl.BlockSpec((None, rows, 128), lambda b, g, i: (g, 0, 0)),
                  table, table,
                  pl.BlockSpec((None, None, n_cmp, HEAD_DIM), lambda b, g, i: (g, b, 0, 0)),
                  pl.BlockSpec((None, None, n_cmp, HEAD_DIM), lambda b, g, i: (NSA_KV + g, b, 0, 0)),
                  kv_col(0), kv_col(1), kv_col(2), kv_col(3)],
        out_specs=pl.BlockSpec((tq, gw), lambda b, g, i: (b * nq + i, g)),
        out_shape=jax.ShapeDtypeStruct((total, NSA_W), BF16),
        scratch_shapes=[pltpu.VMEM((rows, HEAD_DIM), F32)],
        compiler_params=_cparams("parallel", "parallel", "arbitrary"),
        name="nsa_prompt",
    )(q, gates, slopes, coef, tab_s, tab_w, kcvc, kcvc, kv_bf, kv_bf, kv_bf, kv_bf)


def _after_matrix(n):
    si = lax.broadcasted_iota(jnp.int32, (n, n), 0)
    sj = lax.broadcasted_iota(jnp.int32, (n, n), 1)
    return jnp.where(si > sj, 1.0, 0.0).astype(BF16)


def _stick_block(z, carry, after, mask=None):
    log_keep_all = -(jnp.maximum(z, 0.0) + jnp.log(1.0 + jnp.exp(-jnp.abs(z))))
    log_keep = log_keep_all if mask is None else jnp.where(mask, log_keep_all, 0.0)
    hi = log_keep.astype(BF16)
    lo = (log_keep - hi.astype(F32)).astype(BF16)
    later = _dot(hi, after) + _dot(lo, after)
    a = jnp.exp(z + log_keep_all + later + carry)
    if mask is not None:
        a = jnp.where(mask, a, 0.0)
    return a, carry + jnp.sum(log_keep, axis=-1, keepdims=True)


def _sb_prompt_kernel(q_ref, k_ref, v_ref, o_ref, *, tq):
    i = pl.program_id(2)
    n_blk = tq // SB_BLOCK
    t_pos = i * tq + lax.broadcasted_iota(jnp.int32, (tq, 1), 0)
    col = lax.broadcasted_iota(jnp.int32, (1, SB_BLOCK), 1)
    after = _after_matrix(SB_BLOCK)
    heads = [slice(h * HEAD_DIM, (h + 1) * HEAD_DIM) for h in range(SB_HEADS_PER_STEP)]
    qs = [q_ref[:, hd] for hd in heads]

    def chunk(j, carries, diagonal):
        k0 = pl.multiple_of(j * tq, tq)
        outs, new_carries = [], []
        for q, hd, carry in zip(qs, heads, carries):
            z = _dot_nt(q, k_ref[pl.ds(k0, tq), hd].astype(BF16))
            parts = [None] * n_blk
            for s in reversed(range(n_blk)):
                mask = (k0 + s * SB_BLOCK + col) < t_pos if diagonal else None
                parts[s], carry = _stick_block(z[:, s * SB_BLOCK:(s + 1) * SB_BLOCK], carry, after, mask)
            a = jnp.concatenate(parts, axis=1).astype(BF16)
            outs.append(_dot(a, v_ref[pl.ds(k0, tq), hd].astype(BF16)))
            new_carries.append(carry)
        return outs, new_carries

    def carry_max(carries):
        return functools.reduce(jnp.maximum, [jnp.max(c) for c in carries])

    accs, carries = chunk(i, [jnp.zeros((tq, 1), F32)] * SB_HEADS_PER_STEP, True)

    def cond(state):
        j, _, _, cmax = state
        return (j >= 0) & (cmax > EXP_UNDERFLOW)

    def body(state):
        j, carries, accs, _ = state
        outs, carries = chunk(j, carries, False)
        return j - 1, carries, [a + o for a, o in zip(accs, outs)], carry_max(carries)

    _, _, accs, _ = lax.while_loop(cond, body, (i - 1, carries, accs, carry_max(carries)))
    for hd, acc in zip(heads, accs):
        o_ref[:, hd] = acc.astype(o_ref.dtype)


def _sb_prompt(q, sb_rows, n_seq):
    total = q.shape[0]
    seq = total // n_seq
    tq = 256
    nq = seq // tq
    hw = SB_HEADS_PER_STEP * HEAD_DIM
    n_hg = SB_HEADS // SB_HEADS_PER_STEP
    return pl.pallas_call(
        functools.partial(_sb_prompt_kernel, tq=tq),
        grid=(n_seq, n_hg, nq),
        in_specs=[pl.BlockSpec((tq, hw), lambda b, h, i: (b * nq + i, h)),
                  pl.BlockSpec((seq, hw), lambda b, h, i: (b, h)),
                  pl.BlockSpec((seq, hw), lambda b, h, i: (b, n_hg + h))],
        out_specs=pl.BlockSpec((tq, hw), lambda b, h, i: (b * nq + i, h)),
        out_shape=jax.ShapeDtypeStruct((total, SB_W), BF16),
        compiler_params=_cparams("parallel", "parallel", "arbitrary"),
        name="sb_prompt",
    )(q, sb_rows, sb_rows)


def _compress_paged_kernel(pt_ref, cache_ref, w1_ref, pe_ref, w2_ref, kg_ref, o_ref, buf, sem, nxt_ref, *, n_grp):
    n = pl.program_id(0)
    n_steps = pl.num_programs(0)
    pages = CMP_PAGES_PER_STEP
    page_rows = buf.shape[1] // pages
    chunk_rows = CMP_STRIDE * 2 * NSA_KV
    n_chunk = buf.shape[1] // chunk_rows
    half = CMP_STRIDE * HEAD_DIM

    def copies(step, slot):
        b = step // n_grp
        p0 = (n_grp - 1 - step % n_grp) * pages
        return [pltpu.make_async_copy(cache_ref.at[pt_ref[b, p0 + p]], buf.at[slot, pl.ds(p * page_rows, page_rows)],
                                      sem.at[slot]) for p in range(pages)]

    @pl.when(n == 0)
    def _():
        for c in copies(0, 0):
            c.start()

    @pl.when(n + 1 < n_steps)
    def _():
        for c in copies(n + 1, (n + 1) % 2):
            c.start()

    slot = n % 2
    for c in copies(n, slot):
        c.wait()

    @pl.when(n % n_grp == 0)
    def _():
        nxt_ref[...] = jnp.zeros_like(nxt_ref)

    last = lax.broadcasted_iota(jnp.int32, (n_chunk, 1), 0) == n_chunk - 1
    for kv in range(2):
        xs = []
        for g in range(NSA_KV):
            c = kv * NSA_KV + g
            pieces = [buf[slot, pl.ds(r * 2 * NSA_KV + c, n_chunk, stride=chunk_rows), :].astype(BF16)
                      for r in range(CMP_STRIDE)]
            xs.append(jnp.concatenate(pieces, axis=1))
        x = jnp.concatenate(xs, axis=0)
        w1a = w1_ref[kv, 0:half, :]
        w1b = w1_ref[kv, half:2 * half, :]
        pe = pe_ref[kv]
        bias = (_dot(pe[:, :half], w1a) + _dot(pe[:, half:], w1b))[0:1]
        left = _dot(x, w1a)
        right = _dot(x, w1b)
        for g in range(NSA_KV):
            c = kv * NSA_KV + g
            right_g = right[g * n_chunk:(g + 1) * n_chunk]
            nxt = jnp.where(last, nxt_ref[c, 0:1, :], pltpu.roll(right_g, n_chunk - 1, 0))
            pre = left[g * n_chunk:(g + 1) * n_chunk] + nxt + bias
            out = _dot(_gelu(pre).astype(BF16), w2_ref[kv])
            o_ref[c] = _rms(out, kg_ref[...]) if kv == 0 else out
            nxt_ref[c] = jnp.broadcast_to(right_g[0:1], nxt_ref.shape[1:])


def _compress_paged(cache, page_table, pe, w1, w2, k_gain):
    n_seq, n_pages = page_table.shape
    pages = CMP_PAGES_PER_STEP
    assert n_pages % pages == 0
    n_grp = n_pages // pages
    page_rows = cache.shape[1]
    d = cache.shape[2]
    half = CMP_STRIDE * d
    n_chunk = pages * PAGE_SIZE // CMP_STRIDE
    pe_flat = jnp.broadcast_to(pe.reshape(2, 1, 2 * half), (2, 8, 2 * half)).astype(BF16)
    const3 = lambda n, pt: (0, 0, 0)
    grid_spec = pltpu.PrefetchScalarGridSpec(
        num_scalar_prefetch=1,
        grid=(n_seq * n_grp,),
        in_specs=[pl.BlockSpec(memory_space=pl.ANY),
                  pl.BlockSpec((2, 2 * half, CMP_HID), const3),
                  pl.BlockSpec((2, 8, 2 * half), const3),
                  pl.BlockSpec((2, CMP_HID, d), const3),
                  pl.BlockSpec((1, d), lambda n, pt: (0, 0))],
        out_specs=pl.BlockSpec((2 * NSA_KV, None, n_chunk, d),
                               lambda n, pt: (0, n // n_grp, n_grp - 1 - n % n_grp, 0)),
        scratch_shapes=[pltpu.VMEM((2, pages * page_rows, d), F32), pltpu.SemaphoreType.DMA((2,)),
                        pltpu.VMEM((2 * NSA_KV, 8, CMP_HID), F32)])
    return pl.pallas_call(
        functools.partial(_compress_paged_kernel, n_grp=n_grp),
        grid_spec=grid_spec,
        out_shape=jax.ShapeDtypeStruct((2 * NSA_KV, n_seq, n_grp * n_chunk, d), F32),
        compiler_params=_cparams("arbitrary"),
        name="compress_paged",
    )(page_table, cache, w1.astype(BF16), pe_flat, w2.astype(BF16), k_gain.reshape(1, d))


HEAD_ROWS = 8


def _nsa_sample_kernel(pt_ref, q_ref, gate_ref, slope_ref, kc_ref, vc_ref, seln_ref, winn_ref, ws_ref,
                       cache_ref, o_ref, selbuf, kband, vband, sem, *, past):
    b = pl.program_id(0)
    g = pl.program_id(1)
    q = q_ref[...]
    slope = slope_ref[...]
    n_cmp = kc_ref.shape[0]
    n_past_blk = past // SEL_LEN
    n_sel = n_past_blk + 1
    lanes = -(-n_sel // 128) * 128
    k_top = min(TOP_N, n_sel)
    blk_per_page = PAGE_SIZE // SEL_LEN

    kv_rows = 2 * NSA_KV
    blk_rows = SEL_LEN * kv_rows

    def attend(k, v, dist, mask):
        s = _dot_nt(q, k.astype(BF16)) - slope * dist.astype(F32)
        p = _masked_softmax(s, mask)
        return p, _dot(p.astype(BF16), v.astype(BF16))

    c_end = lax.broadcasted_iota(jnp.int32, (1, n_cmp), 1) * CMP_STRIDE + (CMP_LEN - 1)
    p_c, o_c = attend(kc_ref[...], vc_ref[...], past - c_end, c_end <= past)

    is_head = lax.broadcasted_iota(jnp.int32, (HEAD_ROWS, 1), 0) < NSA_GROUP
    p_sum = jnp.broadcast_to(jnp.sum(jnp.where(is_head, p_c, 0.0), axis=0, keepdims=True), (HEAD_ROWS, n_cmp))
    ci = lax.broadcasted_iota(jnp.int32, (n_cmp, lanes), 0)
    bj = lax.broadcasted_iota(jnp.int32, (n_cmp, lanes), 1)
    ratio = SEL_LEN // CMP_STRIDE
    overlap = jnp.where((ci >= ratio * bj - 1) & (ci <= ratio * bj + ratio - 1) & (bj < n_sel), 1.0, 0.0).astype(BF16)
    imp = sum(_dot(part, overlap) for part in _split3(p_sum))[0:1]
    blk = lax.broadcasted_iota(jnp.int32, (1, lanes), 1)
    blk_f = blk.astype(F32)
    cur = past // SEL_LEN
    forced = (blk == 0) | (blk == cur) | (blk == cur - 1)
    score = jnp.where(forced, FORCE_SCORE, jnp.where(blk * SEL_LEN <= past, imp, -1.0))
    score = jnp.where(blk < n_sel, score, -2.0)

    def sel_copy(idx, n):
        page = pt_ref[b, idx // blk_per_page]
        rows = pl.ds(pl.multiple_of((idx % blk_per_page) * blk_rows, blk_rows), blk_rows)
        return pltpu.make_async_copy(cache_ref.at[page, rows], selbuf.at[pl.ds(n * blk_rows, blk_rows)], sem.at[0])

    is_new_pos = lax.broadcasted_iota(jnp.int32, (blk_rows, 1), 0) < kv_rows
    new_rows = jnp.concatenate([seln_ref[...]] * SEL_LEN, axis=0)
    picked = []
    for n in range(k_top):
        best = jnp.max(score)
        idx = jnp.min(jnp.where(score == best, blk_f, float(lanes))).astype(jnp.int32)
        score = jnp.where(blk == idx, NEG_INF, score)
        picked.append(idx)

        @pl.when(idx < n_past_blk)
        def _(idx=idx, n=n):
            sel_copy(idx, n).start()

        @pl.when(idx >= n_past_blk)
        def _(n=n):
            selbuf[n * blk_rows:(n + 1) * blk_rows, :] = jnp.where(is_new_pos, new_rows, 0.0)

    wb = ws_ref.shape[0] // kv_rows
    pad_rows = kband.shape[0] - wb
    is_new_row = lax.broadcasted_iota(jnp.int32, (pad_rows, 1), 0) == 0
    kband[0:wb, :] = ws_ref[pl.ds(g, wb, stride=kv_rows), :]
    vband[0:wb, :] = ws_ref[pl.ds(NSA_KV + g, wb, stride=kv_rows), :]
    kband[wb:wb + pad_rows, :] = jnp.where(is_new_row, winn_ref[pl.ds(g, 1), :], 0.0)
    vband[wb:wb + pad_rows, :] = jnp.where(is_new_row, winn_ref[pl.ds(NSA_KV + g, 1), :], 0.0)
    dist_w = wb - lax.broadcasted_iota(jnp.int32, (1, wb + pad_rows), 1)
    _, o_w = attend(kband[...], vband[...], dist_w, (dist_w >= 0) & (dist_w <= WINDOW))

    for n, idx in enumerate(picked):
        @pl.when(idx < n_past_blk)
        def _(idx=idx, n=n):
            sel_copy(idx, n).wait()

    lane = lax.broadcasted_iota(jnp.int32, (1, k_top * SEL_LEN), 1)
    pos = jnp.zeros((1, k_top * SEL_LEN), jnp.int32)
    for n, idx in enumerate(picked):
        pos = jnp.where(lane // SEL_LEN == n, idx * SEL_LEN + lane - n * SEL_LEN, pos)
    dist_s = past - pos
    k_sel = selbuf[pl.ds(g, k_top * SEL_LEN, stride=kv_rows), :]
    v_sel = selbuf[pl.ds(NSA_KV + g, k_top * SEL_LEN, stride=kv_rows), :]
    _, o_s = attend(k_sel, v_sel, dist_s, dist_s >= 0)

    gates = gate_ref[...]
    o_ref[...] = gates[:, 0:1] * o_c + gates[:, 1:2] * o_s + gates[:, 2:3] * o_w


def _nsa_sample(page_table, q, gates, kcvc, sel_new, win_new, win_state, cache_sel):
    n_seq, n_pages = page_table.shape
    past = n_pages * PAGE_SIZE
    n_cmp = kcvc.shape[2]
    wb = win_state.shape[1] // (2 * NSA_KV)
    d = HEAD_DIM
    k_top = min(TOP_N, past // SEL_LEN + 1)
    slopes = np.ones((NSA_KV, HEAD_ROWS, 1), np.float32)
    slopes[:, :NSA_GROUP, 0] = _alibi_slopes()
    head_blk = lambda: pl.BlockSpec((None, None, HEAD_ROWS, d), lambda b, g, pt: (b, g, 0, 0))
    new_blk = lambda: pl.BlockSpec((None, 2 * NSA_KV, d), lambda b, g, pt: (b, 0, 0))
    grid_spec = pltpu.PrefetchScalarGridSpec(
        num_scalar_prefetch=1,
        grid=(n_seq, NSA_KV),
        in_specs=[head_blk(), head_blk(),
                  pl.BlockSpec((None, HEAD_ROWS, 1), lambda b, g, pt: (g, 0, 0)),
                  pl.BlockSpec((None, None, n_cmp, d), lambda b, g, pt: (g, b, 0, 0)),
                  pl.BlockSpec((None, None, n_cmp, d), lambda b, g, pt: (NSA_KV + g, b, 0, 0)),
                  new_blk(), new_blk(),
                  pl.BlockSpec((None, wb * 2 * NSA_KV, d), lambda b, g, pt: (b, 0, 0)),
                  pl.BlockSpec(memory_space=pl.ANY)],
        out_specs=head_blk(),
        scratch_shapes=[pltpu.VMEM((k_top * SEL_LEN * 2 * NSA_KV, d), F32),
                        pltpu.VMEM((wb + 128, d), F32), pltpu.VMEM((wb + 128, d), F32),
                        pltpu.SemaphoreType.DMA((1,))])
    return pl.pallas_call(
        functools.partial(_nsa_sample_kernel, past=past),
        grid_spec=grid_spec,
        out_shape=jax.ShapeDtypeStruct((n_seq, NSA_KV, HEAD_ROWS, d), F32),
        compiler_params=_cparams("arbitrary", "arbitrary"),
        name="nsa_sample",
    )(page_table, q, gates, jnp.asarray(slopes), kcvc, kcvc, sel_new, win_new, win_state, cache_sel)


def _sb_sample_kernel(pt_ref, q_ref, cache_ref, o_ref, buf, sem):
    b = pl.program_id(0)
    n_pages = pt_ref.shape[1]
    q = q_ref[...]
    head = lax.broadcasted_iota(jnp.int32, (SB_HEADS, 1), 0)
    after = _after_matrix(PAGE_SIZE)
    kv_rows = 2 * SB_HEADS

    def page_copy(p, slot):
        return pltpu.make_async_copy(cache_ref.at[pt_ref[b, p]], buf.at[slot], sem.at[slot])

    def slot_of(p):
        return (n_pages - 1 - p) % 2

    page_copy(n_pages - 1, 0).start()

    def cond(state):
        p, _, _, carry_max = state
        return (p >= 0) & (carry_max > EXP_UNDERFLOW)

    def body(state):
        p, carry, acc, _ = state
        slot = slot_of(p)

        @pl.when(p > 0)
        def _():
            page_copy(p - 1, 1 - slot).start()

        page_copy(p, slot).wait()
        z = jnp.zeros((SB_HEADS, PAGE_SIZE), F32)
        for h in range(SB_HEADS):
            k_h = buf[slot, pl.ds(h, PAGE_SIZE, stride=kv_rows), :].astype(BF16)
            z = jnp.where(head == h, _dot_nt(q, k_h), z)
        a, carry = _stick_block(z, carry, after)
        a = a.astype(BF16)
        for h in range(SB_HEADS):
            v_h = buf[slot, pl.ds(SB_HEADS + h, PAGE_SIZE, stride=kv_rows), :].astype(BF16)
            acc = acc + jnp.where(head == h, _dot(a, v_h), 0.0)
        return p - 1, carry, acc, jnp.max(carry)

    init = (n_pages - 1, jnp.zeros((SB_HEADS, 1), F32), jnp.zeros((SB_HEADS, HEAD_DIM), F32), jnp.float32(0.0))
    p, _, acc, _ = lax.while_loop(cond, body, init)

    @pl.when(p >= 0)
    def _():
        page_copy(p, slot_of(p)).wait()

    o_ref[...] = acc


def _sb_sample(page_table, q, cache):
    n_seq = page_table.shape[0]
    page_rows, d = cache.shape[1:]
    blk = lambda: pl.BlockSpec((None, SB_HEADS, d), lambda b, pt: (b, 0, 0))
    grid_spec = pltpu.PrefetchScalarGridSpec(
        num_scalar_prefetch=1,
        grid=(n_seq,),
        in_specs=[blk(), pl.BlockSpec(memory_space=pl.ANY)],
        out_specs=blk(),
        scratch_shapes=[pltpu.VMEM((2, page_rows, d), F32), pltpu.SemaphoreType.DMA((2,))])
    return pl.pallas_call(
        _sb_sample_kernel,
        grid_spec=grid_spec,
        out_shape=jax.ShapeDtypeStruct((n_seq, SB_HEADS, d), F32),
        compiler_params=_cparams("arbitrary"),
        name="sb_sample",
    )(page_table, q, cache)


def _split_w_in(w_in):
    w = w_in.astype(BF16)
    sizes = (NSA_W, KV_W, KV_W, KV_W, KV_W, KV_W, KV_W, GATE_W, SB_W, SB_W, SB_W, D_MODEL, D_MODEL)
    offs = np.concatenate([[0], np.cumsum(sizes)]).tolist()
    col = lambda a, b: w[:, offs[a]:offs[b]]
    w_q = jnp.concatenate([col(0, 1), col(8, 9)], axis=1)
    w_kv = jnp.concatenate([col(1, 7), jnp.pad(col(7, 8), ((0, 0), (0, GATE_PAD - GATE_W)))], axis=1)
    return w_q, w_kv, col(9, 11), col(11, 12), col(12, 13)


def _peer_keys(sub_keys):
    half = PEER_DK // 2
    z = jnp.zeros((N_KEYS, half), sub_keys.dtype)
    return jnp.concatenate([jnp.concatenate([sub_keys[0], z], axis=1),
                            jnp.concatenate([z, sub_keys[1]], axis=1)], axis=0).astype(BF16)


def kernel(x_prompt, x_sample, cache_cmp_kv, cache_sel_kv, cache_sb_kv, state_win_kv, page_table, norm1_gain, w_in,
           nsa_q_gain, nsa_k_gain, cmp_pe, cmp_w1, cmp_w2, w_up_nsa, w_up_sb, w_out, norm2_gain, peer_w_q,
           peer_sub_keys, peer_u, peer_v):
    assert w_in.shape[0] == 1, "single layer"
    B, T, D = x_prompt.shape
    S, TS, _ = x_sample.shape
    w_q, w_kv, w_sbkv, w_ga, w_gb = _split_w_in(w_in[0])
    w_un, w_us, w_o = w_up_nsa[0].astype(BF16), w_up_sb[0].astype(BF16), w_out[0].astype(BF16)
    wq_peer, keys_bd = peer_w_q[0].astype(BF16), _peer_keys(peer_sub_keys[0])
    u_tab, v_tab = peer_u[0].astype(BF16), peer_v[0].astype(BF16)
    k_gain = nsa_k_gain[0]

    def front(x2d):
        h = _rmsnorm(x2d, norm1_gain[0])
        q, sbq = _proj_q(h, w_q, nsa_q_gain[0])
        cmp_rows, sel_rows, win_rows, gates, cmp_hm, kv_bf = _proj_kv(h, w_kv, k_gain)
        sb_rows = _proj_sbkv(h, w_sbkv)
        return h, q, sbq, cmp_rows, sel_rows, win_rows, gates, cmp_hm, kv_bf, sb_rows

    def back(x2d, h, nsa_out, sb_out):
        mixed = _mix(h, nsa_out, sb_out, w_ga, w_gb, w_un, w_us)
        y, hn = _out_proj(x2d, mixed, w_o, norm2_gain[0])
        return _peer(y, hn, wq_peer, keys_bd, u_tab, v_tab)

    xp = x_prompt.reshape(B * T, D)
    h, q, sbq, cmp_rows, sel_rows, win_rows, gates, cmp_hm, kv_bf, sb_rows = front(xp)
    kcvc = _compress(cmp_hm, cmp_pe[0], cmp_w1[0], cmp_w2[0], k_gain[0], B)
    nsa_out = _nsa_prompt(q, gates, kcvc, kv_bf, B)
    sb_out = _sb_prompt(sbq, sb_rows, B)
    y_p = back(xp, h, nsa_out, sb_out).reshape(B, T, D)
    kv5 = lambda rows, n, t, heads: rows.reshape(1, n, t, 2, heads, HEAD_DIM)
    cmp_p, sel_p, sb_p = kv5(cmp_rows, B, T, NSA_KV), kv5(sel_rows, B, T, NSA_KV), kv5(sb_rows, B, T, SB_HEADS)
    win_p = kv5(win_rows, B, T, NSA_KV)[:, :, -min(WINDOW, T):]

    assert TS == 1 and state_win_kv.shape[2] == WINDOW, "one new token per sequence after a past of >= WINDOW rows"
    n_phys = cache_cmp_kv.shape[1]
    xs = jnp.pad(x_sample.reshape(S, D), ((0, -S % SAMPLE_ROW_PAD), (0, 0)))
    h, *rest = front(xs)
    q, sbq, cmp_rows, sel_rows, win_rows, gates, _, _, sb_rows = (a[..., :S, :] for a in rest)
    cmp_s, sel_s, sb_s = kv5(cmp_rows, S, TS, NSA_KV), kv5(sel_rows, S, TS, NSA_KV), kv5(sb_rows, S, TS, SB_HEADS)
    kcvc_s = _compress_paged(cache_cmp_kv[0].reshape(n_phys, PAGE_SIZE * 2 * NSA_KV, HEAD_DIM), page_table,
                             cmp_pe[0], cmp_w1[0], cmp_w2[0], k_gain[0])
    head_pad = ((0, 0), (0, 0), (0, HEAD_ROWS - NSA_GROUP), (0, 0))
    q_s = jnp.pad(q.reshape(S, NSA_KV, NSA_GROUP, HEAD_DIM), head_pad)
    gate_s = jnp.pad(gates[:, :GATE_W].reshape(S, NSA_KV, NSA_GROUP, 3), head_pad[:3] + ((0, 128 - 3),))
    nsa_s = _nsa_sample(page_table, q_s, gate_s, kcvc_s, sel_rows.reshape(S, 2 * NSA_KV, HEAD_DIM),
                        win_rows.reshape(S, 2 * NSA_KV, HEAD_DIM), state_win_kv[0].reshape(S, WINDOW * 2 * NSA_KV, HEAD_DIM),
                        cache_sel_kv[0].reshape(n_phys, PAGE_SIZE * 2 * NSA_KV, HEAD_DIM))
    sb_out_s = _sb_sample(page_table, sbq.reshape(S, SB_HEADS, HEAD_DIM),
                          cache_sb_kv[0].reshape(n_phys, PAGE_SIZE * 2 * SB_HEADS, HEAD_DIM))
    pad_rows = lambda a: jnp.pad(a, ((0, xs.shape[0] - S), (0, 0)))
    y_s = back(xs, h, pad_rows(nsa_s[:, :, :NSA_GROUP].reshape(S, NSA_W).astype(BF16)),
               pad_rows(sb_out_s.reshape(S, SB_W).astype(BF16)))
    y_s = y_s[:S].reshape(S, TS, D)
    new_win = jnp.concatenate([state_win_kv[:, :, 1:], kv5(win_rows, S, TS, NSA_KV)], axis=2)
    return (y_p, y_s, cmp_p, sel_p, sb_p, win_p, cmp_s, sel_s, sb_s, new_win)
```

```python
import functools

import numpy as np
import jax
import jax.numpy as jnp
from jax import lax
from jax.experimental import pallas as pl
from jax.experimental.pallas import tpu as pltpu

D_MODEL = 2048
PAGE_SIZE = 128
HEAD_DIM = 128
NSA_HEADS = 8
NSA_KV = 2
NSA_GROUP = NSA_HEADS // NSA_KV
NSA_W = NSA_HEADS * HEAD_DIM
KV_W = NSA_KV * HEAD_DIM
CMP_LEN = 32
CMP_STRIDE = 16
CMP_HID = 256
SEL_LEN = 64
TOP_N = 16
WINDOW = 512
FORCE_SCORE = 1.0e4
SB_HEADS = 8
SB_W = SB_HEADS * HEAD_DIM
PEER_HEADS = 8
PEER_TOPK = 16
N_KEYS = 128
N_EXPERTS = N_KEYS * N_KEYS
PEER_DK = 128
NORM_EPS = 1e-6
GATE_W = 3 * NSA_HEADS
GATE_PAD = 128
ATT_SCALE = HEAD_DIM ** -0.5
SAMPLE_ROW_PAD = 128
SEL_PREFIX_STEP = 512
SB_BLOCK = 128
SB_HEADS_PER_STEP = 4
EXP_UNDERFLOW = -104.0
MASK_BIAS = -1.0e30
DIST_SHIFT = 6
DIST_SPLIT = 1 << DIST_SHIFT
CMP_PAGES_PER_STEP = 32

F32 = jnp.float32
BF16 = jnp.bfloat16
NEG_INF = float("-inf")

VMEM_LIMIT_BYTES = 56 * 1024 * 1024


def _cparams(*sem):
    return pltpu.CompilerParams(dimension_semantics=sem, vmem_limit_bytes=VMEM_LIMIT_BYTES)


def _rms(x, gain):
    return x * lax.rsqrt(jnp.mean(x * x, axis=-1, keepdims=True) + NORM_EPS) * gain


def _gelu(x):
    return 0.5 * x * (1.0 + jnp.tanh(0.7978845608028654 * (x + 0.044715 * (x * x * x))))


def _masked_softmax(s, mask):
    sm = jnp.where(mask, s, NEG_INF)
    m = jnp.max(sm, axis=-1, keepdims=True)
    m = jnp.where(m > NEG_INF, m, 0.0)
    e = jnp.where(mask, jnp.exp(s - m), 0.0)
    den = jnp.sum(e, axis=-1, keepdims=True)
    return e / jnp.where(den > 0, den, 1.0)


def _dot(a, b):
    return jnp.dot(a, b, preferred_element_type=F32)


def _dot_nt(a, b):
    return lax.dot_general(a, b, (((1,), (1,)), ((), ())), preferred_element_type=F32)


def _row_tile(m, pref):
    return pref if m % pref == 0 else m


def _rmsnorm_kernel(x_ref, g_ref, o_ref):
    o_ref[...] = _rms(x_ref[...], g_ref[...]).astype(o_ref.dtype)


def _rmsnorm(x, gain):
    m, d = x.shape
    tm = _row_tile(m, 512)
    return pl.pallas_call(
        _rmsnorm_kernel,
        grid=(m // tm,),
        in_specs=[pl.BlockSpec((tm, d), lambda i: (i, 0)), pl.BlockSpec((1, d), lambda i: (0, 0))],
        out_specs=pl.BlockSpec((tm, d), lambda i: (i, 0)),
        out_shape=jax.ShapeDtypeStruct((m, d), BF16),
        compiler_params=_cparams("parallel"),
        name="rmsnorm1",
    )(x, gain.reshape(1, d))


def _proj_q_kernel(h_ref, w_ref, qg_ref, q_ref, sbq_ref):
    acc = _dot(h_ref[...], w_ref[...])
    for hd in range(NSA_HEADS):
        sl = slice(hd * HEAD_DIM, (hd + 1) * HEAD_DIM)
        q_ref[:, sl] = (_rms(acc[:, sl], qg_ref[...]) * ATT_SCALE).astype(BF16)
    sbq_ref[...] = (acc[:, NSA_W:] * ATT_SCALE).astype(BF16)


def _proj_q(h, w, q_gain):
    m, d = h.shape
    tm = _row_tile(m, 512)
    n = NSA_W + SB_W
    return pl.pallas_call(
        _proj_q_kernel,
        grid=(m // tm,),
        in_specs=[pl.BlockSpec((tm, d), lambda i: (i, 0)), pl.BlockSpec((d, n), lambda i: (0, 0)),
                  pl.BlockSpec((1, HEAD_DIM), lambda i: (0, 0))],
        out_specs=[pl.BlockSpec((tm, NSA_W), lambda i: (i, 0)), pl.BlockSpec((tm, SB_W), lambda i: (i, 0))],
        out_shape=[jax.ShapeDtypeStruct((m, NSA_W), BF16), jax.ShapeDtypeStruct((m, SB_W), BF16)],
        compiler_params=_cparams("parallel"),
        name="proj_q",
    )(h, w, q_gain.reshape(1, HEAD_DIM))


def _proj_kv_kernel(h_ref, w_ref, kg_ref, cmp_ref, sel_ref, win_ref, gate_ref, cmph_ref, kvb_ref):
    acc = _dot(h_ref[...], w_ref[...])
    tm = acc.shape[0]
    row_w = 2 * KV_W
    n_c = 2 * NSA_KV
    for c in range(n_c):
        raw = acc[:, c * HEAD_DIM:(c + 1) * HEAD_DIM]
        cmp_ref[pl.ds(c, tm, stride=n_c), :] = raw
        cmph_ref[c] = raw.astype(BF16)
    for r, out_ref in ((1, sel_ref), (2, win_ref)):
        for c in range(n_c):
            val = acc[:, r * row_w + c * HEAD_DIM:r * row_w + (c + 1) * HEAD_DIM]
            if c < NSA_KV:
                val = _rms(val, kg_ref[r:r + 1, :])
            out_ref[pl.ds(c, tm, stride=n_c), :] = val
            kvb_ref[:, (r - 1) * row_w + c * HEAD_DIM:(r - 1) * row_w + (c + 1) * HEAD_DIM] = val.astype(BF16)
    gate_ref[...] = jax.nn.sigmoid(acc[:, 3 * row_w:])


def _proj_kv(h, w, k_gain):
    m, d = h.shape
    tm = _row_tile(m, 512)
    n = 6 * KV_W + GATE_PAD
    row_w = 2 * KV_W
    n_c = 2 * NSA_KV
    row_spec = pl.BlockSpec((tm * n_c, HEAD_DIM), lambda i: (i, 0))
    return pl.pallas_call(
        _proj_kv_kernel,
        grid=(m // tm,),
        in_specs=[pl.BlockSpec((tm, d), lambda i: (i, 0)), pl.BlockSpec((d, n), lambda i: (0, 0)),
                  pl.BlockSpec((3, HEAD_DIM), lambda i: (0, 0))],
        out_specs=[row_spec, row_spec, row_spec, pl.BlockSpec((tm, GATE_PAD), lambda i: (i, 0)),
                   pl.BlockSpec((2 * NSA_KV, tm, HEAD_DIM), lambda i: (0, i, 0)),
                   pl.BlockSpec((tm, 2 * row_w), lambda i: (i, 0))],
        out_shape=[jax.ShapeDtypeStruct((m * n_c, HEAD_DIM), F32)] * 3
        + [jax.ShapeDtypeStruct((m, GATE_PAD), F32), jax.ShapeDtypeStruct((2 * NSA_KV, m, HEAD_DIM), BF16),
           jax.ShapeDtypeStruct((m, 2 * row_w), BF16)],
        compiler_params=_cparams("parallel"),
        name="proj_kv",
    )(h, w, k_gain)


def _proj_sbkv_kernel(h_ref, w_ref, o_ref):
    o_ref[...] = _dot(h_ref[...], w_ref[...])


def _proj_sbkv(h, w):
    m, d = h.shape
    tm = _row_tile(m, 512)
    n = 2 * SB_W
    return pl.pallas_call(
        _proj_sbkv_kernel,
        grid=(m // tm,),
        in_specs=[pl.BlockSpec((tm, d), lambda i: (i, 0)), pl.BlockSpec((d, n), lambda i: (0, 0))],
        out_specs=pl.BlockSpec((tm, n), lambda i: (i, 0)),
        out_shape=jax.ShapeDtypeStruct((m, n), F32),
        compiler_params=_cparams("parallel"),
        name="proj_sbkv",
    )(h, w)


def _mix_kernel(h_ref, a_ref, b_ref, wga_ref, wgb_ref, wa_ref, wb_ref, o_ref):
    h = h_ref[...]
    ga = jax.nn.sigmoid(_dot(h, wga_ref[...]))
    gb = jax.nn.sigmoid(_dot(h, wgb_ref[...]))
    mixed = ga * _dot(a_ref[...], wa_ref[...]) + gb * _dot(b_ref[...], wb_ref[...])
    o_ref[...] = mixed.astype(o_ref.dtype)


def _mix(h, nsa_out, sb_out, w_ga, w_gb, w_up_nsa, w_up_sb):
    m, d = h.shape
    tm = _row_tile(m, 512)
    tn = 512
    return pl.pallas_call(
        _mix_kernel,
        grid=(d // tn, m // tm),
        in_specs=[pl.BlockSpec((tm, d), lambda j, i: (i, 0)),
                  pl.BlockSpec((tm, NSA_W), lambda j, i: (i, 0)),
                  pl.BlockSpec((tm, SB_W), lambda j, i: (i, 0)),
                  pl.BlockSpec((d, tn), lambda j, i: (0, j)),
                  pl.BlockSpec((d, tn), lambda j, i: (0, j)),
                  pl.BlockSpec((NSA_W, tn), lambda j, i: (0, j)),
                  pl.BlockSpec((SB_W, tn), lambda j, i: (0, j))],
        out_specs=pl.BlockSpec((tm, tn), lambda j, i: (i, j)),
        out_shape=jax.ShapeDtypeStruct((m, d), BF16),
        compiler_params=_cparams("parallel", "parallel"),
        name="mix",
    )(h, nsa_out, sb_out, w_ga, w_gb, w_up_nsa, w_up_sb)


def _out_proj_kernel(x_ref, mixed_ref, w_ref, g_ref, y_ref, hn_ref):
    y = x_ref[...] + _dot(mixed_ref[...], w_ref[...])
    y_ref[...] = y
    hn_ref[...] = _rms(y, g_ref[...]).astype(BF16)


def _out_proj(x, mixed, w_out, norm2_gain):
    m, d = x.shape
    tm = _row_tile(m, 512)
    row = pl.BlockSpec((tm, d), lambda i: (i, 0))
    return pl.pallas_call(
        _out_proj_kernel,
        grid=(m // tm,),
        in_specs=[row, row, pl.BlockSpec((d, d), lambda i: (0, 0)), pl.BlockSpec((1, d), lambda i: (0, 0))],
        out_specs=[row, row],
        out_shape=[jax.ShapeDtypeStruct((m, d), F32), jax.ShapeDtypeStruct((m, d), BF16)],
        compiler_params=_cparams("parallel"),
        name="out_proj",
    )(x, mixed, w_out, norm2_gain.reshape(1, d))


def _topk_rows(s, k, payloads=()):
    n = s.shape[0]
    iota = lax.broadcasted_iota(jnp.int32, s.shape, 0).astype(F32)
    vals, idxs = [], []
    picked = [[] for _ in payloads]
    for _ in range(k):
        m = jnp.max(s, axis=0, keepdims=True)
        idx = jnp.min(jnp.where(s == m, iota, float(n)), axis=0, keepdims=True)
        hit = iota == idx
        vals.append(m)
        idxs.append(idx)
        for p, acc in zip(payloads, picked):
            acc.append(jnp.sum(jnp.where(hit, p, 0.0), axis=0, keepdims=True))
        s = jnp.where(hit, NEG_INF, s)
    cat = lambda xs: jnp.concatenate(xs, axis=0)
    return cat(vals), cat(idxs), [cat(p) for p in picked]


_PEER_PAIRS = [(a, b) for a in range(PEER_TOPK) for b in range(PEER_TOPK) if (a + 1) * (b + 1) <= PEER_TOPK]


def _peer_route_kernel(hn_ref, wq_ref, keys_ref, g_ref, ia_ref, ib_ref):
    q = _dot(hn_ref[...], wq_ref[...]).astype(BF16)
    g_rows, a_rows, b_rows = [], [], []
    for hd in range(PEER_HEADS):
        qh = q[:, hd * PEER_DK:(hd + 1) * PEER_DK]
        s = _dot_nt(keys_ref[...], qh)
        s1, i1, _ = _topk_rows(s[:N_KEYS], PEER_TOPK)
        s2, i2, _ = _topk_rows(s[N_KEYS:], PEER_TOPK)
        cand = jnp.concatenate([s1[a:a + 1] + s2[b:b + 1] for a, b in _PEER_PAIRS], axis=0)
        pa = jnp.concatenate([i1[a:a + 1] for a, _ in _PEER_PAIRS], axis=0)
        pb = jnp.concatenate([i2[b:b + 1] for _, b in _PEER_PAIRS], axis=0)
        best, _, (ea, eb) = _topk_rows(cand, PEER_TOPK, (pa, pb))
        e = jnp.exp(best - best[0:1])
        g_rows.append(e / jnp.sum(e, axis=0, keepdims=True))
        a_rows.append(ea)
        b_rows.append(eb)
    g_ref[...] = jnp.concatenate(g_rows, axis=0).T
    ia_ref[...] = jnp.concatenate(a_rows, axis=0).T
    ib_ref[...] = jnp.concatenate(b_rows, axis=0).T


def _peer_route(hn, w_q, keys_bd):
    m, d = hn.shape
    tm = _row_tile(m, 256)
    n_slot = PEER_HEADS * PEER_TOPK
    slot = pl.BlockSpec((tm, n_slot), lambda i: (i, 0))
    return pl.pallas_call(
        _peer_route_kernel,
        grid=(m // tm,),
        in_specs=[pl.BlockSpec((tm, d), lambda i: (i, 0)),
                  pl.BlockSpec((d, PEER_HEADS * PEER_DK), lambda i: (0, 0)),
                  pl.BlockSpec((2 * N_KEYS, PEER_DK), lambda i: (0, 0))],
        out_specs=[slot, slot, slot],
        out_shape=[jax.ShapeDtypeStruct((m, n_slot), F32)] * 3,
        compiler_params=_cparams("parallel"),
        name="peer_route",
    )(hn, w_q, keys_bd)


EXPAND_GROUP = 16
EXPAND_PITCH = N_KEYS + 4


def _peer_expand_kernel(g_ref, ia_ref, ib_ref, w_ref, stage_ref):
    tm = g_ref.shape[0]
    row_id = lax.broadcasted_iota(jnp.int32, (N_KEYS, g_ref.shape[1]), 0).astype(F32)

    def fill(t0, slot):
        for t in range(EXPAND_GROUP):
            g = g_ref[pl.ds(t0 + t, 1), :]
            rt = jnp.where(row_id == ia_ref[pl.ds(t0 + t, 1), :], g, 0.0).astype(BF16)
            ct = jnp.where(row_id == ib_ref[pl.ds(t0 + t, 1), :], 1.0, 0.0).astype(BF16)
            stage_ref[slot, pl.ds(t * EXPAND_PITCH, N_KEYS), :] = _dot_nt(rt, ct)

    def drain(t0, slot):
        for a in range(N_KEYS):
            halves = [stage_ref[slot, pl.ds(a + h * 8 * EXPAND_PITCH, 8, stride=EXPAND_PITCH), :]
                      for h in range(EXPAND_GROUP // 8)]
            w_ref[pl.ds(t0, EXPAND_GROUP), a * N_KEYS:(a + 1) * N_KEYS] = (
                jnp.concatenate(halves, axis=0).astype(w_ref.dtype))

    def pair(gi, carry):
        t0 = pl.multiple_of(gi * 2 * EXPAND_GROUP, 2 * EXPAND_GROUP)
        fill(t0, 0)
        fill(t0 + EXPAND_GROUP, 1)
        drain(t0, 0)
        drain(t0 + EXPAND_GROUP, 1)
        return carry

    lax.fori_loop(0, tm // (2 * EXPAND_GROUP), pair, 0)


def _peer_expand(g, ia, ib):
    m, n_slot = g.shape
    tm = _row_tile(m, 128)
    slot = pl.BlockSpec((tm, n_slot), lambda i: (i, 0))
    return pl.pallas_call(
        _peer_expand_kernel,
        grid=(m // tm,),
        in_specs=[slot, slot, slot],
        out_specs=pl.BlockSpec((tm, N_EXPERTS), lambda i: (i, 0)),
        out_shape=jax.ShapeDtypeStruct((m, N_EXPERTS), BF16),
        scratch_shapes=[pltpu.VMEM((2, EXPAND_GROUP * EXPAND_PITCH, N_KEYS), F32)],
        compiler_params=_cparams("parallel"),
        name="peer_expand",
    )(g, ia, ib)


def _peer_dense_kernel(y_ref, hn_ref, w_ref, u_ref, v_ref, o_ref):
    @pl.when(pl.program_id(1) == 0)
    def _():
        o_ref[...] = y_ref[...]

    act = _dot_nt(hn_ref[...], u_ref[...])
    coeff = (w_ref[...].astype(F32) * _gelu(act)).astype(BF16)
    o_ref[...] += _dot(coeff, v_ref[...])


def _peer_dense(y, hn, w, u_tab, v_tab):
    m, d = y.shape
    tm = _row_tile(m, 1024)
    te = 512
    row = pl.BlockSpec((tm, d), lambda i, j: (i, 0))
    tab = pl.BlockSpec((te, d), lambda i, j: (j, 0))
    return pl.pallas_call(
        _peer_dense_kernel,
        grid=(m // tm, N_EXPERTS // te),
        in_specs=[row, row, pl.BlockSpec((tm, te), lambda i, j: (i, j)), tab, tab],
        out_specs=row,
        out_shape=jax.ShapeDtypeStruct((m, d), F32),
        compiler_params=_cparams("parallel", "arbitrary"),
        name="peer_dense",
    )(y, hn, w, u_tab, v_tab)


def _peer(y, hn, w_q, keys_bd, u_tab, v_tab):
    g, ia, ib = _peer_route(hn, w_q, keys_bd)
    return _peer_dense(y, hn, _peer_expand(g, ia, ib), u_tab, v_tab)


def _compress_kernel(ch_ref, w1a_ref, w1b_ref, pe_ref, w2_ref, kg_ref, o_ref):
    ch = ch_ref[...]
    n_chunk = ch.shape[0]
    half = w1a_ref.shape[0]
    pe = pe_ref[...]
    bias = _dot(pe[:, :half], w1a_ref[...]) + _dot(pe[:, half:], w1b_ref[...])
    nxt = pltpu.roll(_dot(ch, w1b_ref[...]), n_chunk - 1, 0)
    pre = _dot(ch, w1a_ref[...]) + nxt + bias[0:1]
    out = _dot(_gelu(pre).astype(BF16), w2_ref[...])
    is_key = pl.program_id(0) < NSA_KV
    o_ref[...] = jnp.where(is_key, _rms(out, kg_ref[...]), out)


def _compress(rows_hm, pe, w1, w2, k_gain, n_seq):
    n_c, total, d = rows_hm.shape
    n_chunk = total // n_seq // CMP_STRIDE
    half = CMP_STRIDE * d
    ch = rows_hm.reshape(n_c, n_seq, n_chunk, half)
    pe_flat = jnp.broadcast_to(pe.reshape(2, 1, 2 * half), (2, 8, 2 * half)).astype(BF16)
    w1 = w1.astype(BF16)
    kv = lambda c, b: (c // NSA_KV, 0, 0)
    return pl.pallas_call(
        _compress_kernel,
        grid=(n_c, n_seq),
        in_specs=[pl.BlockSpec((None, None, n_chunk, half), lambda c, b: (c, b, 0, 0)),
                  pl.BlockSpec((None, half, CMP_HID), kv),
                  pl.BlockSpec((None, half, CMP_HID), lambda c, b: (c // NSA_KV, 1, 0)),
                  pl.BlockSpec((None, 8, 2 * half), kv),
                  pl.BlockSpec((None, CMP_HID, d), kv),
                  pl.BlockSpec((1, d), lambda c, b: (0, 0))],
        out_specs=pl.BlockSpec((None, None, n_chunk, d), lambda c, b: (c, b, 0, 0)),
        out_shape=jax.ShapeDtypeStruct((n_c, n_seq, n_chunk, d), F32),
        compiler_params=_cparams("parallel", "parallel"),
        name="compress",
    )(ch, w1, w1, pe_flat, w2.astype(BF16), k_gain.reshape(1, d))


def _split3(x):
    a = x.astype(BF16)
    r = x - a.astype(F32)
    b = r.astype(BF16)
    c = (r - b.astype(F32)).astype(BF16)
    return a, b, c


def _nsa_prompt_kernel(q_ref, gate_ref, slope_ref, coef_ref, tab_s_ref, tab_w_ref, kc_ref, vc_ref, ks_ref, vs_ref,
                       kw_ref, vw_ref, o_ref, os_ref, *, tq):
    g = pl.program_id(1)
    cur = pl.program_id(2)
    t0 = cur * tq
    seq = ks_ref.shape[0]
    n_cmp = kc_ref.shape[0]
    n_sel = seq // SEL_LEN
    rel0 = seq - tq
    q4 = jnp.concatenate([q_ref[:, j * HEAD_DIM:(j + 1) * HEAD_DIM] for j in range(NSA_GROUP)], axis=0)
    slope = slope_ref[...]
    tpos = t0 + lax.broadcasted_iota(jnp.int32, (tq, 1), 0)
    tpos4 = jnp.concatenate([tpos] * NSA_GROUP, axis=0)
    coef = coef_ref[...]

    def attend_biased(lhs, k, bias_keys, v):
        s = _dot_nt(lhs, jnp.concatenate([k, bias_keys], axis=1))
        e = jnp.exp(s - jnp.max(s, axis=-1, keepdims=True))
        den = jnp.sum(e, axis=-1, keepdims=True)
        return _dot(e.astype(BF16), v) * (1.0 / den)

    c_end = lax.broadcasted_iota(jnp.int32, (1, n_cmp), 1) * CMP_STRIDE + (CMP_LEN - 1)
    s_c = _dot_nt(q4, kc_ref[...].astype(BF16)) - slope * (tpos4 - c_end).astype(F32)
    p_c = _masked_softmax(s_c, c_end <= tpos4)
    o_c = _dot(p_c.astype(BF16), vc_ref[...].astype(BF16))

    p_sum = p_c[0:tq]
    for j in range(1, NSA_GROUP):
        p_sum = p_sum + p_c[j * tq:(j + 1) * tq]
    ci = lax.broadcasted_iota(jnp.int32, (n_cmp, 128), 0)
    bj = lax.broadcasted_iota(jnp.int32, (n_cmp, 128), 1)
    ratio = SEL_LEN // CMP_STRIDE
    overlap = jnp.where((ci >= ratio * bj - 1) & (ci <= ratio * bj + ratio - 1) & (bj < n_sel), 1.0, 0.0).astype(BF16)
    imp = sum(_dot(part, overlap) for part in _split3(p_sum))

    blk = lax.broadcasted_iota(jnp.int32, (tq, 128), 1)
    forced = (blk == 0) | (blk == cur) | (blk == cur - 1)
    score = jnp.where(forced, FORCE_SCORE, jnp.where(blk * SEL_LEN <= tpos, imp, -1.0))
    score = jnp.where(blk < n_sel, score, -2.0)
    score_t = score.T[0:n_sel, :]
    blk_t = lax.broadcasted_iota(jnp.int32, (n_sel, tq), 0)
    rank_t = jnp.zeros((n_sel, tq), F32)
    for i in range(n_sel):
        row = score_t[i:i + 1, :]
        rank_t = rank_t + jnp.where((row > score_t) | ((row == score_t) & (blk_t > i)), 1.0, 0.0)
    in_top = jnp.where(rank_t < float(min(TOP_N, n_sel)), 1.0, 0.0)
    in_top = jnp.concatenate([in_top, jnp.zeros((128 - n_sel, tq), F32)], axis=0).T
    chosen = (in_top > 0.5) & (blk <= cur)
    not_chosen = jnp.where(chosen, 0.0, MASK_BIAS).astype(BF16)
    fi = lax.broadcasted_iota(jnp.int32, (128, 128), 0)
    fj = lax.broadcasted_iota(jnp.int32, (128, 128), 1)
    flip = jnp.where(fi + fj == cur, 1.0, 0.0).astype(BF16)
    not_chosen_rel = _dot(not_chosen, flip).astype(BF16)
    lane = lax.broadcasted_iota(jnp.int32, coef.shape, 1)
    coef_sel = jnp.where(lane < n_sel, jnp.concatenate([not_chosen_rel] * NSA_GROUP, axis=0), coef)
    tab_row = pl.multiple_of(rel0 - t0, 16)

    def sel_branch(n_keys):
        os_ref[...] = attend_biased(jnp.concatenate([q4, coef_sel], axis=1), ks_ref[0:n_keys, :],
                                    tab_s_ref[pl.ds(tab_row, n_keys), :], vs_ref[0:n_keys, :])

    step = min(SEL_PREFIX_STEP, seq)
    for n_keys in range(step, seq + 1, step):
        pl.when((t0 + tq - 1) // step == n_keys // step - 1)(functools.partial(sel_branch, n_keys))
    o_s = os_ref[...]

    span = min(WINDOW + tq, seq)
    k0 = pl.multiple_of(jnp.clip(t0 - WINDOW, 0, seq - span), 16)
    o_w = attend_biased(jnp.concatenate([q4, coef], axis=1), kw_ref[pl.ds(k0, span), :],
                        tab_w_ref[pl.ds(pl.multiple_of(tab_row + k0, 16), span), :], vw_ref[pl.ds(k0, span), :])

    gates = jnp.where(g == 0, gate_ref[:, 0:3 * NSA_GROUP], gate_ref[:, 3 * NSA_GROUP:6 * NSA_GROUP])
    for j in range(NSA_GROUP):
        r = slice(j * tq, (j + 1) * tq)
        out = (gates[:, 3 * j:3 * j + 1] * o_c[r] + gates[:, 3 * j + 1:3 * j + 2] * o_s[r]
               + gates[:, 3 * j + 2:3 * j + 3] * o_w[r])
        o_ref[:, j * HEAD_DIM:(j + 1) * HEAD_DIM] = out.astype(o_ref.dtype)


def _alibi_slopes():
    h = np.arange(1, NSA_HEADS + 1, dtype=np.float32)
    return np.exp2(-8.0 * h / NSA_HEADS).reshape(NSA_KV, NSA_GROUP)


def _nsa_prompt_consts(tq, seq):
    n_sel = seq // SEL_LEN
    assert tq == SEL_LEN and n_sel + tq + 2 <= 128
    slopes = np.repeat(_alibi_slopes(), tq, axis=1)
    rows = slopes.shape[1]
    coef = np.zeros((NSA_KV, rows, 128), np.float32)
    coef[:, np.arange(rows), n_sel + np.arange(rows) % tq] = 1.0
    coef[:, :, n_sel + tq] = -slopes * DIST_SPLIT
    coef[:, :, n_sel + tq + 1] = -slopes
    rel = np.arange(-(-(2 * seq - tq) // 16) * 16)[:, None] - (seq - tq)
    r = np.arange(tq)[None, :]
    dist = tq - rel[:, 0]

    def table(window):
        tab = np.zeros((rel.shape[0], 128), np.float32)
        tab[:, :n_sel] = (rel < tq) & (-(rel // SEL_LEN) == np.arange(n_sel)[None, :])
        unreadable = (rel > r) | ((r - rel > WINDOW) if window else False)
        tab[:, n_sel:n_sel + tq] = np.where(unreadable, MASK_BIAS, 0.0)
        tab[:, n_sel + tq] = dist >> DIST_SHIFT
        tab[:, n_sel + tq + 1] = dist & (DIST_SPLIT - 1)
        return jnp.asarray(tab, dtype=BF16)

    return jnp.asarray(slopes[..., None]), jnp.asarray(coef, dtype=BF16), table(False), table(True)


def _nsa_prompt(q, gates, kcvc, kv_bf, n_seq):
    total = q.shape[0]
    seq = total // n_seq
    tq = SEL_LEN
    nq = seq // tq
    n_cmp = kcvc.shape[2]
    gw = NSA_GROUP * HEAD_DIM
    rows = NSA_GROUP * tq
    slopes, coef, tab_s, tab_w = _nsa_prompt_consts(tq, seq)
    kv_col = lambda c: pl.BlockSpec((seq, HEAD_DIM), lambda b, g, i: (b, c * NSA_KV + g))
    table = pl.BlockSpec(tab_s.shape, lambda b, g, i: (0, 0))
    return pl.pallas_call(
        functools.partial(_nsa_prompt_kernel, tq=tq),
        grid=(n_seq, NSA_KV, nq),
        in_specs=[pl.BlockSpec((tq, gw), lambda b, g, i: (b * nq + i, g)),
                  pl.BlockSpec((tq, GATE_PAD), lambda b, g, i: (b * nq + i, 0)),
                  pl.BlockSpec((None, rows, 1), lambda b, g, i: (g, 0, 0)),
                  pl.BlockSpec((None, rows, 128), lambda b, g, i: (g, 0, 0)),
                  table, table,
                  pl.BlockSpec((None, None, n_cmp, HEAD_DIM), lambda b, g, i: (g, b, 0, 0)),
                  pl.BlockSpec((None, None, n_cmp, HEAD_DIM), lambda b, g, i: (NSA_KV + g, b, 0, 0)),
                  kv_col(0), kv_col(1), kv_col(2), kv_col(3)],
        out_specs=pl.BlockSpec((tq, gw), lambda b, g, i: (b * nq + i, g)),
        out_shape=jax.ShapeDtypeStruct((total, NSA_W), BF16),
        scratch_shapes=[pltpu.VMEM((rows, HEAD_DIM), F32)],
        compiler_params=_cparams("parallel", "parallel", "arbitrary"),
        name="nsa_prompt",
    )(q, gates, slopes, coef, tab_s, tab_w, kcvc, kcvc, kv_bf, kv_bf, kv_bf, kv_bf)


def _after_matrix(n):
    si = lax.broadcasted_iota(jnp.int32, (n, n), 0)
    sj = lax.broadcasted_iota(jnp.int32, (n, n), 1)
    return jnp.where(si > sj, 1.0, 0.0).astype(BF16)


def _stick_block(z, carry, after, mask=None):
    log_keep_all = -(jnp.maximum(z, 0.0) + jnp.log(1.0 + jnp.exp(-jnp.abs(z))))
    log_keep = log_keep_all if mask is None else jnp.where(mask, log_keep_all, 0.0)
    hi = log_keep.astype(BF16)
    lo = (log_keep - hi.astype(F32)).astype(BF16)
    later = _dot(hi, after) + _dot(lo, after)
    a = jnp.exp(z + log_keep_all + later + carry)
    if mask is not None:
        a = jnp.where(mask, a, 0.0)
    return a, carry + jnp.sum(log_keep, axis=-1, keepdims=True)


def _sb_prompt_kernel(q_ref, k_ref, v_ref, o_ref, *, tq):
    i = pl.program_id(2)
    n_blk = tq // SB_BLOCK
    t_pos = i * tq + lax.broadcasted_iota(jnp.int32, (tq, 1), 0)
    col = lax.broadcasted_iota(jnp.int32, (1, SB_BLOCK), 1)
    after = _after_matrix(SB_BLOCK)
    heads = [slice(h * HEAD_DIM, (h + 1) * HEAD_DIM) for h in range(SB_HEADS_PER_STEP)]
    qs = [q_ref[:, hd] for hd in heads]

    def chunk(j, carries, diagonal):
        k0 = pl.multiple_of(j * tq, tq)
        outs, new_carries = [], []
        for q, hd, carry in zip(qs, heads, carries):
            z = _dot_nt(q, k_ref[pl.ds(k0, tq), hd].astype(BF16))
            parts = [None] * n_blk
            for s in reversed(range(n_blk)):
                mask = (k0 + s * SB_BLOCK + col) < t_pos if diagonal else None
                parts[s], carry = _stick_block(z[:, s * SB_BLOCK:(s + 1) * SB_BLOCK], carry, after, mask)
            a = jnp.concatenate(parts, axis=1).astype(BF16)
            outs.append(_dot(a, v_ref[pl.ds(k0, tq), hd].astype(BF16)))
            new_carries.append(carry)
        return outs, new_carries

    def carry_max(carries):
        return functools.reduce(jnp.maximum, [jnp.max(c) for c in carries])

    accs, carries = chunk(i, [jnp.zeros((tq, 1), F32)] * SB_HEADS_PER_STEP, True)

    def cond(state):
        j, _, _, cmax = state
        return (j >= 0) & (cmax > EXP_UNDERFLOW)

    def body(state):
        j, carries, accs, _ = state
        outs, carries = chunk(j, carries, False)
        return j - 1, carries, [a + o for a, o in zip(accs, outs)], carry_max(carries)

    _, _, accs, _ = lax.while_loop(cond, body, (i - 1, carries, accs, carry_max(carries)))
    for hd, acc in zip(heads, accs):
        o_ref[:, hd] = acc.astype(o_ref.dtype)


def _sb_prompt(q, sb_rows, n_seq):
    total = q.shape[0]
    seq = total // n_seq
    tq = 256
    nq = seq // tq
    hw = SB_HEADS_PER_STEP * HEAD_DIM
    n_hg = SB_HEADS // SB_HEADS_PER_STEP
    return pl.pallas_call(
        functools.partial(_sb_prompt_kernel, tq=tq),
        grid=(n_seq, n_hg, nq),
        in_specs=[pl.BlockSpec((tq, hw), lambda b, h, i: (b * nq + i, h)),
                  pl.BlockSpec((seq, hw), lambda b, h, i: (b, h)),
                  pl.BlockSpec((seq, hw), lambda b, h, i: (b, n_hg + h))],
        out_specs=pl.BlockSpec((tq, hw), lambda b, h, i: (b * nq + i, h)),
        out_shape=jax.ShapeDtypeStruct((total, SB_W), BF16),
        compiler_params=_cparams("parallel", "parallel", "arbitrary"),
        name="sb_prompt",
    )(q, sb_rows, sb_rows)


def _compress_paged_kernel(pt_ref, cache_ref, w1_ref, pe_ref, w2_ref, kg_ref, o_ref, buf, sem, nxt_ref, *, n_grp):
    n = pl.program_id(0)
    n_steps = pl.num_programs(0)
    pages = CMP_PAGES_PER_STEP
    page_rows = buf.shape[1] // pages
    chunk_rows = CMP_STRIDE * 2 * NSA_KV
    n_chunk = buf.shape[1] // chunk_rows
    half = CMP_STRIDE * HEAD_DIM

    def copies(step, slot):
        b = step // n_grp
        p0 = (n_grp - 1 - step % n_grp) * pages
        return [pltpu.make_async_copy(cache_ref.at[pt_ref[b, p0 + p]], buf.at[slot, pl.ds(p * page_rows, page_rows)],
                                      sem.at[slot]) for p in range(pages)]

    @pl.when(n == 0)
    def _():
        for c in copies(0, 0):
            c.start()

    @pl.when(n + 1 < n_steps)
    def _():
        for c in copies(n + 1, (n + 1) % 2):
            c.start()

    slot = n % 2
    for c in copies(n, slot):
        c.wait()

    @pl.when(n % n_grp == 0)
    def _():
        nxt_ref[...] = jnp.zeros_like(nxt_ref)

    last = lax.broadcasted_iota(jnp.int32, (n_chunk, 1), 0) == n_chunk - 1
    for kv in range(2):
        xs = []
        for g in range(NSA_KV):
            c = kv * NSA_KV + g
            pieces = [buf[slot, pl.ds(r * 2 * NSA_KV + c, n_chunk, stride=chunk_rows), :].astype(BF16)
                      for r in range(CMP_STRIDE)]
            xs.append(jnp.concatenate(pieces, axis=1))
        x = jnp.concatenate(xs, axis=0)
        w1a = w1_ref[kv, 0:half, :]
        w1b = w1_ref[kv, half:2 * half, :]
        pe = pe_ref[kv]
        bias = (_dot(pe[:, :half], w1a) + _dot(pe[:, half:], w1b))[0:1]
        left = _dot(x, w1a)
        right = _dot(x, w1b)
        for g in range(NSA_KV):
            c = kv * NSA_KV + g
            right_g = right[g * n_chunk:(g + 1) * n_chunk]
            nxt = jnp.where(last, nxt_ref[c, 0:1, :], pltpu.roll(right_g, n_chunk - 1, 0))
            pre = left[g * n_chunk:(g + 1) * n_chunk] + nxt + bias
            out = _dot(_gelu(pre).astype(BF16), w2_ref[kv])
            o_ref[c] = _rms(out, kg_ref[...]) if kv == 0 else out
            nxt_ref[c] = jnp.broadcast_to(right_g[0:1], nxt_ref.shape[1:])


def _compress_paged(cache, page_table, pe, w1, w2, k_gain):
    n_seq, n_pages = page_table.shape
    pages = CMP_PAGES_PER_STEP
    assert n_pages % pages == 0
    n_grp = n_pages // pages
    page_rows = cache.shape[1]
    d = cache.shape[2]
    half = CMP_STRIDE * d
    n_chunk = pages * PAGE_SIZE // CMP_STRIDE
    pe_flat = jnp.broadcast_to(pe.reshape(2, 1, 2 * half), (2, 8, 2 * half)).astype(BF16)
    const3 = lambda n, pt: (0, 0, 0)
    grid_spec = pltpu.PrefetchScalarGridSpec(
        num_scalar_prefetch=1,
        grid=(n_seq * n_grp,),
        in_specs=[pl.BlockSpec(memory_space=pl.ANY),
                  pl.BlockSpec((2, 2 * half, CMP_HID), const3),
                  pl.BlockSpec((2, 8, 2 * half), const3),
                  pl.BlockSpec((2, CMP_HID, d), const3),
                  pl.BlockSpec((1, d), lambda n, pt: (0, 0))],
        out_specs=pl.BlockSpec((2 * NSA_KV, None, n_chunk, d),
                               lambda n, pt: (0, n // n_grp, n_grp - 1 - n % n_grp, 0)),
        scratch_shapes=[pltpu.VMEM((2, pages * page_rows, d), F32), pltpu.SemaphoreType.DMA((2,)),
                        pltpu.VMEM((2 * NSA_KV, 8, CMP_HID), F32)])
    return pl.pallas_call(
        functools.partial(_compress_paged_kernel, n_grp=n_grp),
        grid_spec=grid_spec,
        out_shape=jax.ShapeDtypeStruct((2 * NSA_KV, n_seq, n_grp * n_chunk, d), F32),
        compiler_params=_cparams("arbitrary"),
        name="compress_paged",
    )(page_table, cache, w1.astype(BF16), pe_flat, w2.astype(BF16), k_gain.reshape(1, d))


HEAD_ROWS = 8


def _nsa_sample_kernel(pt_ref, q_ref, gate_ref, slope_ref, kc_ref, vc_ref, seln_ref, winn_ref, ws_ref,
                       cache_ref, o_ref, selbuf, kband, vband, sem, *, past):
    b = pl.program_id(0)
    g = pl.program_id(1)
    q = q_ref[...]
    slope = slope_ref[...]
    n_cmp = kc_ref.shape[0]
    n_past_blk = past // SEL_LEN
    n_sel = n_past_blk + 1
    lanes = -(-n_sel // 128) * 128
    k_top = min(TOP_N, n_sel)
    blk_per_page = PAGE_SIZE // SEL_LEN

    kv_rows = 2 * NSA_KV
    blk_rows = SEL_LEN * kv_rows

    def attend(k, v, dist, mask):
        s = _dot_nt(q, k.astype(BF16)) - slope * dist.astype(F32)
        p = _masked_softmax(s, mask)
        return p, _dot(p.astype(BF16), v.astype(BF16))

    c_end = lax.broadcasted_iota(jnp.int32, (1, n_cmp), 1) * CMP_STRIDE + (CMP_LEN - 1)
    p_c, o_c = attend(kc_ref[...], vc_ref[...], past - c_end, c_end <= past)

    is_head = lax.broadcasted_iota(jnp.int32, (HEAD_ROWS, 1), 0) < NSA_GROUP
    p_sum = jnp.broadcast_to(jnp.sum(jnp.where(is_head, p_c, 0.0), axis=0, keepdims=True), (HEAD_ROWS, n_cmp))
    ci = lax.broadcasted_iota(jnp.int32, (n_cmp, lanes), 0)
    bj = lax.broadcasted_iota(jnp.int32, (n_cmp, lanes), 1)
    ratio = SEL_LEN // CMP_STRIDE
    overlap = jnp.where((ci >= ratio * bj - 1) & (ci <= ratio * bj + ratio - 1) & (bj < n_sel), 1.0, 0.0).astype(BF16)
    imp = sum(_dot(part, overlap) for part in _split3(p_sum))[0:1]
    blk = lax.broadcasted_iota(jnp.int32, (1, lanes), 1)
    blk_f = blk.astype(F32)
    cur = past // SEL_LEN
    forced = (blk == 0) | (blk == cur) | (blk == cur - 1)
    score = jnp.where(forced, FORCE_SCORE, jnp.where(blk * SEL_LEN <= past, imp, -1.0))
    score = jnp.where(blk < n_sel, score, -2.0)

    def sel_copy(idx, n):
        page = pt_ref[b, idx // blk_per_page]
        rows = pl.ds(pl.multiple_of((idx % blk_per_page) * blk_rows, blk_rows), blk_rows)
        return pltpu.make_async_copy(cache_ref.at[page, rows], selbuf.at[pl.ds(n * blk_rows, blk_rows)], sem.at[0])

    is_new_pos = lax.broadcasted_iota(jnp.int32, (blk_rows, 1), 0) < kv_rows
    new_rows = jnp.concatenate([seln_ref[...]] * SEL_LEN, axis=0)
    picked = []
    for n in range(k_top):
        best = jnp.max(score)
        idx = jnp.min(jnp.where(score == best, blk_f, float(lanes))).astype(jnp.int32)
        score = jnp.where(blk == idx, NEG_INF, score)
        picked.append(idx)

        @pl.when(idx < n_past_blk)
        def _(idx=idx, n=n):
            sel_copy(idx, n).start()

        @pl.when(idx >= n_past_blk)
        def _(n=n):
            selbuf[n * blk_rows:(n + 1) * blk_rows, :] = jnp.where(is_new_pos, new_rows, 0.0)

    wb = ws_ref.shape[0] // kv_rows
    pad_rows = kband.shape[0] - wb
    is_new_row = lax.broadcasted_iota(jnp.int32, (pad_rows, 1), 0) == 0
    kband[0:wb, :] = ws_ref[pl.ds(g, wb, stride=kv_rows), :]
    vband[0:wb, :] = ws_ref[pl.ds(NSA_KV + g, wb, stride=kv_rows), :]
    kband[wb:wb + pad_rows, :] = jnp.where(is_new_row, winn_ref[pl.ds(g, 1), :], 0.0)
    vband[wb:wb + pad_rows, :] = jnp.where(is_new_row, winn_ref[pl.ds(NSA_KV + g, 1), :], 0.0)
    dist_w = wb - lax.broadcasted_iota(jnp.int32, (1, wb + pad_rows), 1)
    _, o_w = attend(kband[...], vband[...], dist_w, (dist_w >= 0) & (dist_w <= WINDOW))

    for n, idx in enumerate(picked):
        @pl.when(idx < n_past_blk)
        def _(idx=idx, n=n):
            sel_copy(idx, n).wait()

    lane = lax.broadcasted_iota(jnp.int32, (1, k_top * SEL_LEN), 1)
    pos = jnp.zeros((1, k_top * SEL_LEN), jnp.int32)
    for n, idx in enumerate(picked):
        pos = jnp.where(lane // SEL_LEN == n, idx * SEL_LEN + lane - n * SEL_LEN, pos)
    dist_s = past - pos
    k_sel = selbuf[pl.ds(g, k_top * SEL_LEN, stride=kv_rows), :]
    v_sel = selbuf[pl.ds(NSA_KV + g, k_top * SEL_LEN, stride=kv_rows), :]
    _, o_s = attend(k_sel, v_sel, dist_s, dist_s >= 0)

    gates = gate_ref[...]
    o_ref[...] = gates[:, 0:1] * o_c + gates[:, 1:2] * o_s + gates[:, 2:3] * o_w


def _nsa_sample(page_table, q, gates, kcvc, sel_new, win_new, win_state, cache_sel):
    n_seq, n_pages = page_table.shape
    past = n_pages * PAGE_SIZE
    n_cmp = kcvc.shape[2]
    wb = win_state.shape[1] // (2 * NSA_KV)
    d = HEAD_DIM
    k_top = min(TOP_N, past // SEL_LEN + 1)
    slopes = np.ones((NSA_KV, HEAD_ROWS, 1), np.float32)
    slopes[:, :NSA_GROUP, 0] = _alibi_slopes()
    head_blk = lambda: pl.BlockSpec((None, None, HEAD_ROWS, d), lambda b, g, pt: (b, g, 0, 0))
    new_blk = lambda: pl.BlockSpec((None, 2 * NSA_KV, d), lambda b, g, pt: (b, 0, 0))
    grid_spec = pltpu.PrefetchScalarGridSpec(
        num_scalar_prefetch=1,
        grid=(n_seq, NSA_KV),
        in_specs=[head_blk(), head_blk(),
                  pl.BlockSpec((None, HEAD_ROWS, 1), lambda b, g, pt: (g, 0, 0)),
                  pl.BlockSpec((None, None, n_cmp, d), lambda b, g, pt: (g, b, 0, 0)),
                  pl.BlockSpec((None, None, n_cmp, d), lambda b, g, pt: (NSA_KV + g, b, 0, 0)),
                  new_blk(), new_blk(),
                  pl.BlockSpec((None, wb * 2 * NSA_KV, d), lambda b, g, pt: (b, 0, 0)),
                  pl.BlockSpec(memory_space=pl.ANY)],
        out_specs=head_blk(),
        scratch_shapes=[pltpu.VMEM((k_top * SEL_LEN * 2 * NSA_KV, d), F32),
                        pltpu.VMEM((wb + 128, d), F32), pltpu.VMEM((wb + 128, d), F32),
                        pltpu.SemaphoreType.DMA((1,))])
    return pl.pallas_call(
        functools.partial(_nsa_sample_kernel, past=past),
        grid_spec=grid_spec,
        out_shape=jax.ShapeDtypeStruct((n_seq, NSA_KV, HEAD_ROWS, d), F32),
        compiler_params=_cparams("arbitrary", "arbitrary"),
        name="nsa_sample",
    )(page_table, q, gates, jnp.asarray(slopes), kcvc, kcvc, sel_new, win_new, win_state, cache_sel)


def _sb_sample_kernel(pt_ref, q_ref, cache_ref, o_ref, buf, sem):
    b = pl.program_id(0)
    n_pages = pt_ref.shape[1]
    q = q_ref[...]
    head = lax.broadcasted_iota(jnp.int32, (SB_HEADS, 1), 0)
    after = _after_matrix(PAGE_SIZE)
    kv_rows = 2 * SB_HEADS

    def page_copy(p, slot):
        return pltpu.make_async_copy(cache_ref.at[pt_ref[b, p]], buf.at[slot], sem.at[slot])

    def slot_of(p):
        return (n_pages - 1 - p) % 2

    page_copy(n_pages - 1, 0).start()

    def cond(state):
        p, _, _, carry_max = state
        return (p >= 0) & (carry_max > EXP_UNDERFLOW)

    def body(state):
        p, carry, acc, _ = state
        slot = slot_of(p)

        @pl.when(p > 0)
        def _():
            page_copy(p - 1, 1 - slot).start()

        page_copy(p, slot).wait()
        z = jnp.zeros((SB_HEADS, PAGE_SIZE), F32)
        for h in range(SB_HEADS):
            k_h = buf[slot, pl.ds(h, PAGE_SIZE, stride=kv_rows), :].astype(BF16)
            z = jnp.where(head == h, _dot_nt(q, k_h), z)
        a, carry = _stick_block(z, carry, after)
        a = a.astype(BF16)
        for h in range(SB_HEADS):
            v_h = buf[slot, pl.ds(SB_HEADS + h, PAGE_SIZE, stride=kv_rows), :].astype(BF16)
            acc = acc + jnp.where(head == h, _dot(a, v_h), 0.0)
        return p - 1, carry, acc, jnp.max(carry)

    init = (n_pages - 1, jnp.zeros((SB_HEADS, 1), F32), jnp.zeros((SB_HEADS, HEAD_DIM), F32), jnp.float32(0.0))
    p, _, acc, _ = lax.while_loop(cond, body, init)

    @pl.when(p >= 0)
    def _():
        page_copy(p, slot_of(p)).wait()

    o_ref[...] = acc


def _sb_sample(page_table, q, cache):
    n_seq = page_table.shape[0]
    page_rows, d = cache.shape[1:]
    blk = lambda: pl.BlockSpec((None, SB_HEADS, d), lambda b, pt: (b, 0, 0))
    grid_spec = pltpu.PrefetchScalarGridSpec(
        num_scalar_prefetch=1,
        grid=(n_seq,),
        in_specs=[blk(), pl.BlockSpec(memory_space=pl.ANY)],
        out_specs=blk(),
        scratch_shapes=[pltpu.VMEM((2, page_rows, d), F32), pltpu.SemaphoreType.DMA((2,))])
    return pl.pallas_call(
        _sb_sample_kernel,
        grid_spec=grid_spec,
        out_shape=jax.ShapeDtypeStruct((n_seq, SB_HEADS, d), F32),
        compiler_params=_cparams("arbitrary"),
        name="sb_sample",
    )(page_table, q, cache)


def _split_w_in(w_in):
    w = w_in.astype(BF16)
    sizes = (NSA_W, KV_W, KV_W, KV_W, KV_W, KV_W, KV_W, GATE_W, SB_W, SB_W, SB_W, D_MODEL, D_MODEL)
    offs = np.concatenate([[0], np.cumsum(sizes)]).tolist()
    col = lambda a, b: w[:, offs[a]:offs[b]]
    w_q = jnp.concatenate([col(0, 1), col(8, 9)], axis=1)
    w_kv = jnp.concatenate([col(1, 7), jnp.pad(col(7, 8), ((0, 0), (0, GATE_PAD - GATE_W)))], axis=1)
    return w_q, w_kv, col(9, 11), col(11, 12), col(12, 13)


def _peer_keys(sub_keys):
    half = PEER_DK // 2
    z = jnp.zeros((N_KEYS, half), sub_keys.dtype)
    return jnp.concatenate([jnp.concatenate([sub_keys[0], z], axis=1),
                            jnp.concatenate([z, sub_keys[1]], axis=1)], axis=0).astype(BF16)


def kernel(x_prompt, x_sample, cache_cmp_kv, cache_sel_kv, cache_sb_kv, state_win_kv, page_table, norm1_gain, w_in,
           nsa_q_gain, nsa_k_gain, cmp_pe, cmp_w1, cmp_w2, w_up_nsa, w_up_sb, w_out, norm2_gain, peer_w_q,
           peer_sub_keys, peer_u, peer_v):
    assert w_in.shape[0] == 1, "single layer"
    B, T, D = x_prompt.shape
    S, TS, _ = x_sample.shape
    w_q, w_kv, w_sbkv, w_ga, w_gb = _split_w_in(w_in[0])
    w_un, w_us, w_o = w_up_nsa[0].astype(BF16), w_up_sb[0].astype(BF16), w_out[0].astype(BF16)
    wq_peer, keys_bd = peer_w_q[0].astype(BF16), _peer_keys(peer_sub_keys[0])
    u_tab, v_tab = peer_u[0].astype(BF16), peer_v[0].astype(BF16)
    k_gain = nsa_k_gain[0]

    def front(x2d):
        h = _rmsnorm(x2d, norm1_gain[0])
        q, sbq = _proj_q(h, w_q, nsa_q_gain[0])
        cmp_rows, sel_rows, win_rows, gates, cmp_hm, kv_bf = _proj_kv(h, w_kv, k_gain)
        sb_rows = _proj_sbkv(h, w_sbkv)
        return h, q, sbq, cmp_rows, sel_rows, win_rows, gates, cmp_hm, kv_bf, sb_rows

    def back(x2d, h, nsa_out, sb_out):
        mixed = _mix(h, nsa_out, sb_out, w_ga, w_gb, w_un, w_us)
        y, hn = _out_proj(x2d, mixed, w_o, norm2_gain[0])
        return _peer(y, hn, wq_peer, keys_bd, u_tab, v_tab)

    xp = x_prompt.reshape(B * T, D)
    h, q, sbq, cmp_rows, sel_rows, win_rows, gates, cmp_hm, kv_bf, sb_rows = front(xp)
    kcvc = _compress(cmp_hm, cmp_pe[0], cmp_w1[0], cmp_w2[0], k_gain[0], B)
    nsa_out = _nsa_prompt(q, gates, kcvc, kv_bf, B)
    sb_out = _sb_prompt(sbq, sb_rows, B)
    y_p = back(xp, h, nsa_out, sb_out).reshape(B, T, D)
    kv5 = lambda rows, n, t, heads: rows.reshape(1, n, t, 2, heads, HEAD_DIM)
    cmp_p, sel_p, sb_p = kv5(cmp_rows, B, T, NSA_KV), kv5(sel_rows, B, T, NSA_KV), kv5(sb_rows, B, T, SB_HEADS)
    win_p = kv5(win_rows, B, T, NSA_KV)[:, :, -min(WINDOW, T):]

    assert TS == 1 and state_win_kv.shape[2] == WINDOW, "one new token per sequence after a past of >= WINDOW rows"
    n_phys = cache_cmp_kv.shape[1]
    xs = jnp.pad(x_sample.reshape(S, D), ((0, -S % SAMPLE_ROW_PAD), (0, 0)))
    h, q, sbq, cmp_rows, sel_rows, win_rows, gates, _, _, sb_rows = front(xs)
    q, sbq, gates, sb_rows = (a[:S] for a in (q, sbq, gates, sb_rows))
    cmp_rows, sel_rows, win_rows = (a[:S * 2 * NSA_KV] for a in (cmp_rows, sel_rows, win_rows))
    cmp_s, sel_s, sb_s = kv5(cmp_rows, S, TS, NSA_KV), kv5(sel_rows, S, TS, NSA_KV), kv5(sb_rows, S, TS, SB_HEADS)
    kcvc_s = _compress_paged(cache_cmp_kv[0].reshape(n_phys, PAGE_SIZE * 2 * NSA_KV, HEAD_DIM), page_table,
                             cmp_pe[0], cmp_w1[0], cmp_w2[0], k_gain[0])
    head_pad = ((0, 0), (0, 0), (0, HEAD_ROWS - NSA_GROUP), (0, 0))
    q_s = jnp.pad(q.reshape(S, NSA_KV, NSA_GROUP, HEAD_DIM), head_pad)
    gate_s = jnp.pad(gates[:, :GATE_W].reshape(S, NSA_KV, NSA_GROUP, 3), head_pad[:3] + ((0, 128 - 3),))
    nsa_s = _nsa_sample(page_table, q_s, gate_s, kcvc_s, sel_rows.reshape(S, 2 * NSA_KV, HEAD_DIM),
                        win_rows.reshape(S, 2 * NSA_KV, HEAD_DIM), state_win_kv[0].reshape(S, WINDOW * 2 * NSA_KV, HEAD_DIM),
                        cache_sel_kv[0].reshape(n_phys, PAGE_SIZE * 2 * NSA_KV, HEAD_DIM))
    sb_out_s = _sb_sample(page_table, sbq.reshape(S, SB_HEADS, HEAD_DIM),
                          cache_sb_kv[0].reshape(n_phys, PAGE_SIZE * 2 * SB_HEADS, HEAD_DIM))
    pad_rows = lambda a: jnp.pad(a, ((0, xs.shape[0] - S), (0, 0)))
    y_s = back(xs, h, pad_rows(nsa_s[:, :, :NSA_GROUP].reshape(S, NSA_W).astype(BF16)),
               pad_rows(sb_out_s.reshape(S, SB_W).astype(BF16)))
    y_s = y_s[:S].reshape(S, TS, D)
    new_win = jnp.concatenate([state_win_kv[:, :, 1:], kv5(win_rows, S, TS, NSA_KV)], axis=2)
    return (y_p, y_s, cmp_p, sel_p, sb_p, win_p, cmp_s, sel_s, sb_s, new_win)
```

```python
import functools

import numpy as np
import jax
import jax.numpy as jnp
from jax import lax
from jax.experimental import pallas as pl
from jax.experimental.pallas import tpu as pltpu

D_MODEL = 2048
PAGE_SIZE = 128
HEAD_DIM = 128
NSA_HEADS = 8
NSA_KV = 2
NSA_GROUP = NSA_HEADS // NSA_KV
NSA_W = NSA_HEADS * HEAD_DIM
KV_W = NSA_KV * HEAD_DIM
CMP_LEN = 32
CMP_STRIDE = 16
CMP_HID = 256
SEL_LEN = 64
TOP_N = 16
WINDOW = 512
FORCE_SCORE = 1.0e4
SB_HEADS = 8
SB_W = SB_HEADS * HEAD_DIM
PEER_HEADS = 8
PEER_TOPK = 16
N_KEYS = 128
N_EXPERTS = N_KEYS * N_KEYS
PEER_DK = 128
NORM_EPS = 1e-6
GATE_W = 3 * NSA_HEADS
GATE_PAD = 128
ATT_SCALE = HEAD_DIM ** -0.5
SAMPLE_ROW_PAD = 128
SEL_PREFIX_STEP = 512
SB_BLOCK = 128
SB_HEADS_PER_STEP = 4
EXP_UNDERFLOW = -104.0
MASK_BIAS = -1.0e30
DIST_SHIFT = 6
DIST_SPLIT = 1 << DIST_SHIFT
CMP_PAGES_PER_STEP = 32
CMP_CHUNK_PITCH = 72

F32 = jnp.float32
BF16 = jnp.bfloat16
NEG_INF = float("-inf")

VMEM_LIMIT_BYTES = 56 * 1024 * 1024


def _cparams(*sem):
    return pltpu.CompilerParams(dimension_semantics=sem, vmem_limit_bytes=VMEM_LIMIT_BYTES)


def _rms(x, gain):
    return x * lax.rsqrt(jnp.mean(x * x, axis=-1, keepdims=True) + NORM_EPS) * gain


def _gelu(x):
    return 0.5 * x * (1.0 + jnp.tanh(0.7978845608028654 * (x + 0.044715 * (x * x * x))))


def _masked_softmax(s, mask):
    sm = jnp.where(mask, s, NEG_INF)
    m = jnp.max(sm, axis=-1, keepdims=True)
    m = jnp.where(m > NEG_INF, m, 0.0)
    e = jnp.where(mask, jnp.exp(s - m), 0.0)
    den = jnp.sum(e, axis=-1, keepdims=True)
    return e / jnp.where(den > 0, den, 1.0)


def _dot(a, b):
    return jnp.dot(a, b, preferred_element_type=F32)


def _dot_nt(a, b):
    return lax.dot_general(a, b, (((1,), (1,)), ((), ())), preferred_element_type=F32)


def _row_tile(m, pref):
    return pref if m % pref == 0 else m


def _proj_q_kernel(x_ref, ng_ref, w_ref, qg_ref, h_ref, q_ref, sbq_ref):
    h = _rms(x_ref[...], ng_ref[...]).astype(BF16)
    h_ref[...] = h
    acc = _dot(h, w_ref[...])
    for hd in range(NSA_HEADS):
        sl = slice(hd * HEAD_DIM, (hd + 1) * HEAD_DIM)
        q_ref[:, sl] = (_rms(acc[:, sl], qg_ref[...]) * ATT_SCALE).astype(BF16)
    sbq_ref[...] = (acc[:, NSA_W:] * ATT_SCALE).astype(BF16)


def _proj_q(x, norm_gain, w, q_gain):
    m, d = x.shape
    tm = _row_tile(m, 512)
    n = NSA_W + SB_W
    row = pl.BlockSpec((tm, d), lambda i: (i, 0))
    return pl.pallas_call(
        _proj_q_kernel,
        grid=(m // tm,),
        in_specs=[row, pl.BlockSpec((1, d), lambda i: (0, 0)), pl.BlockSpec((d, n), lambda i: (0, 0)),
                  pl.BlockSpec((1, HEAD_DIM), lambda i: (0, 0))],
        out_specs=[row, pl.BlockSpec((tm, NSA_W), lambda i: (i, 0)), pl.BlockSpec((tm, SB_W), lambda i: (i, 0))],
        out_shape=[jax.ShapeDtypeStruct((m, d), BF16), jax.ShapeDtypeStruct((m, NSA_W), BF16),
                   jax.ShapeDtypeStruct((m, SB_W), BF16)],
        compiler_params=_cparams("parallel"),
        name="proj_q",
    )(x, norm_gain.reshape(1, d), w, q_gain.reshape(1, HEAD_DIM))


def _proj_kv_kernel(h_ref, w_ref, kg_ref, cmp_ref, sel_ref, win_ref, gate_ref, cmph_ref, kvb_ref):
    acc = _dot(h_ref[...], w_ref[...])
    tm = acc.shape[0]
    row_w = 2 * KV_W
    n_c = 2 * NSA_KV
    for c in range(n_c):
        raw = acc[:, c * HEAD_DIM:(c + 1) * HEAD_DIM]
        cmp_ref[pl.ds(c, tm, stride=n_c), :] = raw
        cmph_ref[c] = raw.astype(BF16)
    for r, out_ref in ((1, sel_ref), (2, win_ref)):
        for c in range(n_c):
            val = acc[:, r * row_w + c * HEAD_DIM:r * row_w + (c + 1) * HEAD_DIM]
            if c < NSA_KV:
                val = _rms(val, kg_ref[r:r + 1, :])
            out_ref[pl.ds(c, tm, stride=n_c), :] = val
            kvb_ref[:, (r - 1) * row_w + c * HEAD_DIM:(r - 1) * row_w + (c + 1) * HEAD_DIM] = val.astype(BF16)
    gate_ref[...] = jax.nn.sigmoid(acc[:, 3 * row_w:])


def _proj_kv(h, w, k_gain):
    m, d = h.shape
    tm = _row_tile(m, 512)
    n = 6 * KV_W + GATE_PAD
    row_w = 2 * KV_W
    n_c = 2 * NSA_KV
    row_spec = pl.BlockSpec((tm * n_c, HEAD_DIM), lambda i: (i, 0))
    return pl.pallas_call(
        _proj_kv_kernel,
        grid=(m // tm,),
        in_specs=[pl.BlockSpec((tm, d), lambda i: (i, 0)), pl.BlockSpec((d, n), lambda i: (0, 0)),
                  pl.BlockSpec((3, HEAD_DIM), lambda i: (0, 0))],
        out_specs=[row_spec, row_spec, row_spec, pl.BlockSpec((tm, GATE_PAD), lambda i: (i, 0)),
                   pl.BlockSpec((2 * NSA_KV, tm, HEAD_DIM), lambda i: (0, i, 0)),
                   pl.BlockSpec((tm, 2 * row_w), lambda i: (i, 0))],
        out_shape=[jax.ShapeDtypeStruct((m * n_c, HEAD_DIM), F32)] * 3
        + [jax.ShapeDtypeStruct((m, GATE_PAD), F32), jax.ShapeDtypeStruct((2 * NSA_KV, m, HEAD_DIM), BF16),
           jax.ShapeDtypeStruct((m, 2 * row_w), BF16)],
        compiler_params=_cparams("parallel"),
        name="proj_kv",
    )(h, w, k_gain)


def _proj_sbkv_kernel(h_ref, w_ref, o_ref):
    o_ref[...] = _dot(h_ref[...], w_ref[...])


def _proj_sbkv(h, w):
    m, d = h.shape
    tm = _row_tile(m, 512)
    n = 2 * SB_W
    return pl.pallas_call(
        _proj_sbkv_kernel,
        grid=(m // tm,),
        in_specs=[pl.BlockSpec((tm, d), lambda i: (i, 0)), pl.BlockSpec((d, n), lambda i: (0, 0))],
        out_specs=pl.BlockSpec((tm, n), lambda i: (i, 0)),
        out_shape=jax.ShapeDtypeStruct((m, n), F32),
        compiler_params=_cparams("parallel"),
        name="proj_sbkv",
    )(h, w)


def _mix_kernel(h_ref, a_ref, b_ref, wga_ref, wgb_ref, wa_ref, wb_ref, o_ref):
    h = h_ref[...]
    ga = jax.nn.sigmoid(_dot(h, wga_ref[...]))
    gb = jax.nn.sigmoid(_dot(h, wgb_ref[...]))
    mixed = ga * _dot(a_ref[...], wa_ref[...]) + gb * _dot(b_ref[...], wb_ref[...])
    o_ref[...] = mixed.astype(o_ref.dtype)


def _mix(h, nsa_out, sb_out, w_ga, w_gb, w_up_nsa, w_up_sb):
    m, d = h.shape
    tm = _row_tile(m, 512)
    tn = 512
    return pl.pallas_call(
        _mix_kernel,
        grid=(d // tn, m // tm),
        in_specs=[pl.BlockSpec((tm, d), lambda j, i: (i, 0)),
                  pl.BlockSpec((tm, NSA_W), lambda j, i: (i, 0)),
                  pl.BlockSpec((tm, SB_W), lambda j, i: (i, 0)),
                  pl.BlockSpec((d, tn), lambda j, i: (0, j)),
                  pl.BlockSpec((d, tn), lambda j, i: (0, j)),
                  pl.BlockSpec((NSA_W, tn), lambda j, i: (0, j)),
                  pl.BlockSpec((SB_W, tn), lambda j, i: (0, j))],
        out_specs=pl.BlockSpec((tm, tn), lambda j, i: (i, j)),
        out_shape=jax.ShapeDtypeStruct((m, d), BF16),
        compiler_params=_cparams("parallel", "parallel"),
        name="mix",
    )(h, nsa_out, sb_out, w_ga, w_gb, w_up_nsa, w_up_sb)


def _out_proj_kernel(x_ref, mixed_ref, w_ref, g_ref, y_ref, hn_ref):
    y = x_ref[...] + _dot(mixed_ref[...], w_ref[...])
    y_ref[...] = y
    hn_ref[...] = _rms(y, g_ref[...]).astype(BF16)


def _out_proj(x, mixed, w_out, norm2_gain):
    m, d = x.shape
    tm = _row_tile(m, 512)
    row = pl.BlockSpec((tm, d), lambda i: (i, 0))
    return pl.pallas_call(
        _out_proj_kernel,
        grid=(m // tm,),
        in_specs=[row, row, pl.BlockSpec((d, d), lambda i: (0, 0)), pl.BlockSpec((1, d), lambda i: (0, 0))],
        out_specs=[row, row],
        out_shape=[jax.ShapeDtypeStruct((m, d), F32), jax.ShapeDtypeStruct((m, d), BF16)],
        compiler_params=_cparams("parallel"),
        name="out_proj",
    )(x, mixed, w_out, norm2_gain.reshape(1, d))


def _topk_rows(s, k, payloads=()):
    n = s.shape[0]
    iota = lax.broadcasted_iota(jnp.int32, s.shape, 0).astype(F32)
    vals, idxs = [], []
    picked = [[] for _ in payloads]
    for _ in range(k):
        m = jnp.max(s, axis=0, keepdims=True)
        idx = jnp.min(jnp.where(s == m, iota, float(n)), axis=0, keepdims=True)
        hit = iota == idx
        vals.append(m)
        idxs.append(idx)
        for p, acc in zip(payloads, picked):
            acc.append(jnp.sum(jnp.where(hit, p, 0.0), axis=0, keepdims=True))
        s = jnp.where(hit, NEG_INF, s)
    cat = lambda xs: jnp.concatenate(xs, axis=0)
    return cat(vals), cat(idxs), [cat(p) for p in picked]


_PEER_PAIRS = [(a, b) for a in range(PEER_TOPK) for b in range(PEER_TOPK) if (a + 1) * (b + 1) <= PEER_TOPK]


def _peer_route_kernel(hn_ref, wq_ref, keys_ref, g_ref, ia_ref, ib_ref):
    q = _dot(hn_ref[...], wq_ref[...]).astype(BF16)
    g_rows, a_rows, b_rows = [], [], []
    for hd in range(PEER_HEADS):
        qh = q[:, hd * PEER_DK:(hd + 1) * PEER_DK]
        s = _dot_nt(keys_ref[...], qh)
        s1, i1, _ = _topk_rows(s[:N_KEYS], PEER_TOPK)
        s2, i2, _ = _topk_rows(s[N_KEYS:], PEER_TOPK)
        cand = jnp.concatenate([s1[a:a + 1] + s2[b:b + 1] for a, b in _PEER_PAIRS], axis=0)
        pa = jnp.concatenate([i1[a:a + 1] for a, _ in _PEER_PAIRS], axis=0)
        pb = jnp.concatenate([i2[b:b + 1] for _, b in _PEER_PAIRS], axis=0)
        best, _, (ea, eb) = _topk_rows(cand, PEER_TOPK, (pa, pb))
        e = jnp.exp(best - best[0:1])
        g_rows.append(e / jnp.sum(e, axis=0, keepdims=True))
        a_rows.append(ea)
        b_rows.append(eb)
    g_ref[...] = jnp.concatenate(g_rows, axis=0).T
    ia_ref[...] = jnp.concatenate(a_rows, axis=0).T
    ib_ref[...] = jnp.concatenate(b_rows, axis=0).T


def _peer_route(hn, w_q, keys_bd):
    m, d = hn.shape
    tm = _row_tile(m, 256)
    n_slot = PEER_HEADS * PEER_TOPK
    slot = pl.BlockSpec((tm, n_slot), lambda i: (i, 0))
    return pl.pallas_call(
        _peer_route_kernel,
        grid=(m // tm,),
        in_specs=[pl.BlockSpec((tm, d), lambda i: (i, 0)),
                  pl.BlockSpec((d, PEER_HEADS * PEER_DK), lambda i: (0, 0)),
                  pl.BlockSpec((2 * N_KEYS, PEER_DK), lambda i: (0, 0))],
        out_specs=[slot, slot, slot],
        out_shape=[jax.ShapeDtypeStruct((m, n_slot), F32)] * 3,
        compiler_params=_cparams("parallel"),
        name="peer_route",
    )(hn, w_q, keys_bd)


EXPAND_GROUP = 16
EXPAND_PITCH = N_KEYS + 4


def _peer_expand_kernel(g_ref, ia_ref, ib_ref, w_ref, stage_ref):
    tm = g_ref.shape[0]
    row_id = lax.broadcasted_iota(jnp.int32, (N_KEYS, g_ref.shape[1]), 0).astype(F32)

    def fill(t0, slot):
        for t in range(EXPAND_GROUP):
            g = g_ref[pl.ds(t0 + t, 1), :]
            rt = jnp.where(row_id == ia_ref[pl.ds(t0 + t, 1), :], g, 0.0).astype(BF16)
            ct = jnp.where(row_id == ib_ref[pl.ds(t0 + t, 1), :], 1.0, 0.0).astype(BF16)
            stage_ref[slot, pl.ds(t * EXPAND_PITCH, N_KEYS), :] = _dot_nt(rt, ct)

    def drain(t0, slot):
        for a in range(N_KEYS):
            halves = [stage_ref[slot, pl.ds(a + h * 8 * EXPAND_PITCH, 8, stride=EXPAND_PITCH), :]
                      for h in range(EXPAND_GROUP // 8)]
            w_ref[pl.ds(t0, EXPAND_GROUP), a * N_KEYS:(a + 1) * N_KEYS] = (
                jnp.concatenate(halves, axis=0).astype(w_ref.dtype))

    def pair(gi, carry):
        t0 = pl.multiple_of(gi * 2 * EXPAND_GROUP, 2 * EXPAND_GROUP)
        fill(t0, 0)
        fill(t0 + EXPAND_GROUP, 1)
        drain(t0, 0)
        drain(t0 + EXPAND_GROUP, 1)
        return carry

    lax.fori_loop(0, tm // (2 * EXPAND_GROUP), pair, 0)


def _peer_expand(g, ia, ib):
    m, n_slot = g.shape
    tm = _row_tile(m, 128)
    slot = pl.BlockSpec((tm, n_slot), lambda i: (i, 0))
    return pl.pallas_call(
        _peer_expand_kernel,
        grid=(m // tm,),
        in_specs=[slot, slot, slot],
        out_specs=pl.BlockSpec((tm, N_EXPERTS), lambda i: (i, 0)),
        out_shape=jax.ShapeDtypeStruct((m, N_EXPERTS), BF16),
        scratch_shapes=[pltpu.VMEM((2, EXPAND_GROUP * EXPAND_PITCH, N_KEYS), F32)],
        compiler_params=_cparams("parallel"),
        name="peer_expand",
    )(g, ia, ib)


def _peer_dense_kernel(y_ref, hn_ref, w_ref, u_ref, v_ref, o_ref):
    @pl.when(pl.program_id(1) == 0)
    def _():
        o_ref[...] = y_ref[...]

    act = _dot_nt(hn_ref[...], u_ref[...])
    coeff = (w_ref[...].astype(F32) * _gelu(act)).astype(BF16)
    o_ref[...] += _dot(coeff, v_ref[...])


def _peer_dense(y, hn, w, u_tab, v_tab):
    m, d = y.shape
    tm = _row_tile(m, 1024)
    te = 512
    row = pl.BlockSpec((tm, d), lambda i, j: (i, 0))
    tab = pl.BlockSpec((te, d), lambda i, j: (j, 0))
    return pl.pallas_call(
        _peer_dense_kernel,
        grid=(m // tm, N_EXPERTS // te),
        in_specs=[row, row, pl.BlockSpec((tm, te), lambda i, j: (i, j)), tab, tab],
        out_specs=row,
        out_shape=jax.ShapeDtypeStruct((m, d), F32),
        compiler_params=_cparams("parallel", "arbitrary"),
        name="peer_dense",
    )(y, hn, w, u_tab, v_tab)


def _peer(y, hn, w_q, keys_bd, u_tab, v_tab):
    g, ia, ib = _peer_route(hn, w_q, keys_bd)
    return _peer_dense(y, hn, _peer_expand(g, ia, ib), u_tab, v_tab)


def _compress_kernel(ch_ref, w1a_ref, w1b_ref, pe_ref, w2_ref, kg_ref, o_ref):
    ch = ch_ref[...]
    n_chunk = ch.shape[0]
    half = w1a_ref.shape[0]
    pe = pe_ref[...]
    bias = _dot(pe[:, :half], w1a_ref[...]) + _dot(pe[:, half:], w1b_ref[...])
    nxt = pltpu.roll(_dot(ch, w1b_ref[...]), n_chunk - 1, 0)
    pre = _dot(ch, w1a_ref[...]) + nxt + bias[0:1]
    out = _dot(_gelu(pre).astype(BF16), w2_ref[...])
    is_key = pl.program_id(0) < NSA_KV
    o_ref[...] = jnp.where(is_key, _rms(out, kg_ref[...]), out)


def _compress(rows_hm, pe, w1, w2, k_gain, n_seq):
    n_c, total, d = rows_hm.shape
    n_chunk = total // n_seq // CMP_STRIDE
    half = CMP_STRIDE * d
    ch = rows_hm.reshape(n_c, n_seq, n_chunk, half)
    pe_flat = jnp.broadcast_to(pe.reshape(2, 1, 2 * half), (2, 8, 2 * half)).astype(BF16)
    w1 = w1.astype(BF16)
    kv = lambda c, b: (c // NSA_KV, 0, 0)
    return pl.pallas_call(
        _compress_kernel,
        grid=(n_c, n_seq),
        in_specs=[pl.BlockSpec((None, None, n_chunk, half), lambda c, b: (c, b, 0, 0)),
                  pl.BlockSpec((None, half, CMP_HID), kv),
                  pl.BlockSpec((None, half, CMP_HID), lambda c, b: (c // NSA_KV, 1, 0)),
                  pl.BlockSpec((None, 8, 2 * half), kv),
                  pl.BlockSpec((None, CMP_HID, d), kv),
                  pl.BlockSpec((1, d), lambda c, b: (0, 0))],
        out_specs=pl.BlockSpec((None, None, n_chunk, d), lambda c, b: (c, b, 0, 0)),
        out_shape=jax.ShapeDtypeStruct((n_c, n_seq, n_chunk, d), F32),
        compiler_params=_cparams("parallel", "parallel"),
        name="compress",
    )(ch, w1, w1, pe_flat, w2.astype(BF16), k_gain.reshape(1, d))


def _split3(x):
    a = x.astype(BF16)
    r = x - a.astype(F32)
    b = r.astype(BF16)
    c = (r - b.astype(F32)).astype(BF16)
    return a, b, c


def _nsa_prompt_kernel(q_ref, gate_ref, slope_ref, coef_ref, tab_s_ref, tab_w_ref, kc_ref, vc_ref, ks_ref, vs_ref,
                       kw_ref, vw_ref, o_ref, os_ref, *, tq):
    g = pl.program_id(1)
    cur = pl.program_id(2)
    t0 = cur * tq
    seq = ks_ref.shape[0]
    n_cmp = kc_ref.shape[0]
    n_sel = seq // SEL_LEN
    rel0 = seq - tq
    q4 = jnp.concatenate([q_ref[:, j * HEAD_DIM:(j + 1) * HEAD_DIM] for j in range(NSA_GROUP)], axis=0)
    slope = slope_ref[...]
    tpos = t0 + lax.broadcasted_iota(jnp.int32, (tq, 1), 0)
    tpos4 = jnp.concatenate([tpos] * NSA_GROUP, axis=0)
    coef = coef_ref[...]

    def attend_biased(lhs, k, bias_keys, v):
        s = _dot_nt(lhs, jnp.concatenate([k, bias_keys], axis=1))
        e = jnp.exp(s - jnp.max(s, axis=-1, keepdims=True))
        den = jnp.sum(e, axis=-1, keepdims=True)
        return _dot(e.astype(BF16), v) * (1.0 / den)

    c_end = lax.broadcasted_iota(jnp.int32, (1, n_cmp), 1) * CMP_STRIDE + (CMP_LEN - 1)
    s_c = _dot_nt(q4, kc_ref[...].astype(BF16)) - slope * (tpos4 - c_end).astype(F32)
    p_c = _masked_softmax(s_c, c_end <= tpos4)
    o_c = _dot(p_c.astype(BF16), vc_ref[...].astype(BF16))

    p_sum = p_c[0:tq]
    for j in range(1, NSA_GROUP):
        p_sum = p_sum + p_c[j * tq:(j + 1) * tq]
    ci = lax.broadcasted_iota(jnp.int32, (n_cmp, 128), 0)
    bj = lax.broadcasted_iota(jnp.int32, (n_cmp, 128), 1)
    ratio = SEL_LEN // CMP_STRIDE
    overlap = jnp.where((ci >= ratio * bj - 1) & (ci <= ratio * bj + ratio - 1) & (bj < n_sel), 1.0, 0.0).astype(BF16)
    imp = sum(_dot(part, overlap) for part in _split3(p_sum))

    blk = lax.broadcasted_iota(jnp.int32, (tq, 128), 1)
    forced = (blk == 0) | (blk == cur) | (blk == cur - 1)
    score = jnp.where(forced, FORCE_SCORE, jnp.where(blk * SEL_LEN <= tpos, imp, -1.0))
    score = jnp.where(blk < n_sel, score, -2.0)
    score_t = score.T[0:n_sel, :]
    blk_t = lax.broadcasted_iota(jnp.int32, (n_sel, tq), 0)
    rank_t = jnp.zeros((n_sel, tq), F32)
    for i in range(n_sel):
        row = score_t[i:i + 1, :]
        rank_t = rank_t + jnp.where((row > score_t) | ((row == score_t) & (blk_t > i)), 1.0, 0.0)
    in_top = jnp.where(rank_t < float(min(TOP_N, n_sel)), 1.0, 0.0)
    in_top = jnp.concatenate([in_top, jnp.zeros((128 - n_sel, tq), F32)], axis=0).T
    chosen = (in_top > 0.5) & (blk <= cur)
    not_chosen = jnp.where(chosen, 0.0, MASK_BIAS).astype(BF16)
    fi = lax.broadcasted_iota(jnp.int32, (128, 128), 0)
    fj = lax.broadcasted_iota(jnp.int32, (128, 128), 1)
    flip = jnp.where(fi + fj == cur, 1.0, 0.0).astype(BF16)
    not_chosen_rel = _dot(not_chosen, flip).astype(BF16)
    lane = lax.broadcasted_iota(jnp.int32, coef.shape, 1)
    coef_sel = jnp.where(lane < n_sel, jnp.concatenate([not_chosen_rel] * NSA_GROUP, axis=0), coef)
    tab_row = pl.multiple_of(rel0 - t0, 16)

    def sel_branch(n_keys):
        os_ref[...] = attend_biased(jnp.concatenate([q4, coef_sel], axis=1), ks_ref[0:n_keys, :],
                                    tab_s_ref[pl.ds(tab_row, n_keys), :], vs_ref[0:n_keys, :])

    step = min(SEL_PREFIX_STEP, seq)
    for n_keys in range(step, seq + 1, step):
        pl.when((t0 + tq - 1) // step == n_keys // step - 1)(functools.partial(sel_branch, n_keys))
    o_s = os_ref[...]

    span = min(WINDOW + tq, seq)
    k0 = pl.multiple_of(jnp.clip(t0 - WINDOW, 0, seq - span), 16)
    o_w = attend_biased(jnp.concatenate([q4, coef], axis=1), kw_ref[pl.ds(k0, span), :],
                        tab_w_ref[pl.ds(pl.multiple_of(tab_row + k0, 16), span), :], vw_ref[pl.ds(k0, span), :])

    gates = jnp.where(g == 0, gate_ref[:, 0:3 * NSA_GROUP], gate_ref[:, 3 * NSA_GROUP:6 * NSA_GROUP])
    for j in range(NSA_GROUP):
        r = slice(j * tq, (j + 1) * tq)
        out = (gates[:, 3 * j:3 * j + 1] * o_c[r] + gates[:, 3 * j + 1:3 * j + 2] * o_s[r]
               + gates[:, 3 * j + 2:3 * j + 3] * o_w[r])
        o_ref[:, j * HEAD_DIM:(j + 1) * HEAD_DIM] = out.astype(o_ref.dtype)


def _alibi_slopes():
    h = np.arange(1, NSA_HEADS + 1, dtype=np.float32)
    return np.exp2(-8.0 * h / NSA_HEADS).reshape(NSA_KV, NSA_GROUP)


def _nsa_prompt_consts(tq, seq):
    n_sel = seq // SEL_LEN
    assert tq == SEL_LEN and n_sel + tq + 2 <= 128
    slopes = np.repeat(_alibi_slopes(), tq, axis=1)
    rows = slopes.shape[1]
    coef = np.zeros((NSA_KV, rows, 128), np.float32)
    coef[:, np.arange(rows), n_sel + np.arange(rows) % tq] = 1.0
    coef[:, :, n_sel + tq] = -slopes * DIST_SPLIT
    coef[:, :, n_sel + tq + 1] = -slopes
    rel = np.arange(-(-(2 * seq - tq) // 16) * 16)[:, None] - (seq - tq)
    r = np.arange(tq)[None, :]
    dist = tq - rel[:, 0]

    def table(window):
        tab = np.zeros((rel.shape[0], 128), np.float32)
        tab[:, :n_sel] = (rel < tq) & (-(rel // SEL_LEN) == np.arange(n_sel)[None, :])
        unreadable = (rel > r) | ((r - rel > WINDOW) if window else False)
        tab[:, n_sel:n_sel + tq] = np.where(unreadable, MASK_BIAS, 0.0)
        tab[:, n_sel + tq] = dist >> DIST_SHIFT
        tab[:, n_sel + tq + 1] = dist & (DIST_SPLIT - 1)
        return jnp.asarray(tab, dtype=BF16)

    return jnp.asarray(slopes[..., None]), jnp.asarray(coef, dtype=BF16), table(False), table(True)


def _nsa_prompt(q, gates, kcvc, kv_bf, n_seq):
    total = q.shape[0]
    seq = total // n_seq
    tq = SEL_LEN
    nq = seq // tq
    n_cmp = kcvc.shape[2]
    gw = NSA_GROUP * HEAD_DIM
    rows = NSA_GROUP * tq
    slopes, coef, tab_s, tab_w = _nsa_prompt_consts(tq, seq)
    kv_col = lambda c: pl.BlockSpec((seq, HEAD_DIM), lambda b, g, i: (b, c * NSA_KV + g))
    table = pl.BlockSpec(tab_s.shape, lambda b, g, i: (0, 0))
    return pl.pallas_call(
        functools.partial(_nsa_prompt_kernel, tq=tq),
        grid=(n_seq, NSA_KV, nq),
        in_specs=[pl.BlockSpec((tq, gw), lambda b, g, i: (b * nq + i, g)),
                  pl.BlockSpec((tq, GATE_PAD), lambda b, g, i: (b * nq + i, 0)),
                  pl.BlockSpec((None, rows, 1), lambda b, g, i: (g, 0, 0)),
                  pl.BlockSpec((None, rows, 128), lambda b, g, i: (g, 0, 0)),
                  table, table,
                  pl.BlockSpec((None, None, n_cmp, HEAD_DIM), lambda b, g, i: (g, b, 0, 0)),
                  pl.BlockSpec((None, None, n_cmp, HEAD_DIM), lambda b, g, i: (NSA_KV + g, b, 0, 0)),
                  kv_col(0), kv_col(1), kv_col(2), kv_col(3)],
        out_specs=pl.BlockSpec((tq, gw), lambda b, g, i: (b * nq + i, g)),
        out_shape=jax.ShapeDtypeStruct((total, NSA_W), BF16),
        scratch_shapes=[pltpu.VMEM((rows, HEAD_DIM), F32)],
        compiler_params=_cparams("parallel", "parallel", "arbitrary"),
        name="nsa_prompt",
    )(q, gates, slopes, coef, tab_s, tab_w, kcvc, kcvc, kv_bf, kv_bf, kv_bf, kv_bf)


def _after_matrix(n):
    si = lax.broadcasted_iota(jnp.int32, (n, n), 0)
    sj = lax.broadcasted_iota(jnp.int32, (n, n), 1)
    return jnp.where(si > sj, 1.0, 0.0).astype(BF16)


def _stick_block(z, carry, after, mask=None):
    log_keep_all = -(jnp.maximum(z, 0.0) + jnp.log(1.0 + jnp.exp(-jnp.abs(z))))
    log_keep = log_keep_all if mask is None else jnp.where(mask, log_keep_all, 0.0)
    hi = log_keep.astype(BF16)
    lo = (log_keep - hi.astype(F32)).astype(BF16)
    later = _dot(hi, after) + _dot(lo, after)
    a = jnp.exp(z + log_keep_all + later + carry)
    if mask is not None:
        a = jnp.where(mask, a, 0.0)
    return a, carry + jnp.sum(log_keep, axis=-1, keepdims=True)


def _sb_prompt_kernel(q_ref, k_ref, v_ref, o_ref, *, tq):
    i = pl.program_id(2)
    n_blk = tq // SB_BLOCK
    t_pos = i * tq + lax.broadcasted_iota(jnp.int32, (tq, 1), 0)
    col = lax.broadcasted_iota(jnp.int32, (1, SB_BLOCK), 1)
    after = _after_matrix(SB_BLOCK)
    heads = [slice(h * HEAD_DIM, (h + 1) * HEAD_DIM) for h in range(SB_HEADS_PER_STEP)]
    qs = [q_ref[:, hd] for hd in heads]

    def chunk(j, carries, diagonal):
        k0 = pl.multiple_of(j * tq, tq)
        outs, new_carries = [], []
        for q, hd, carry in zip(qs, heads, carries):
            z = _dot_nt(q, k_ref[pl.ds(k0, tq), hd].astype(BF16))
            parts = [None] * n_blk
            for s in reversed(range(n_blk)):
                mask = (k0 + s * SB_BLOCK + col) < t_pos if diagonal else None
                parts[s], carry = _stick_block(z[:, s * SB_BLOCK:(s + 1) * SB_BLOCK], carry, after, mask)
            a = jnp.concatenate(parts, axis=1).astype(BF16)
            outs.append(_dot(a, v_ref[pl.ds(k0, tq), hd].astype(BF16)))
            new_carries.append(carry)
        return outs, new_carries

    def carry_max(carries):
        return functools.reduce(jnp.maximum, [jnp.max(c) for c in carries])

    accs, carries = chunk(i, [jnp.zeros((tq, 1), F32)] * SB_HEADS_PER_STEP, True)

    def cond(state):
        j, _, _, cmax = state
        return (j >= 0) & (cmax > EXP_UNDERFLOW)

    def body(state):
        j, carries, accs, _ = state
        outs, carries = chunk(j, carries, False)
        return j - 1, carries, [a + o for a, o in zip(accs, outs)], carry_max(carries)

    _, _, accs, _ = lax.while_loop(cond, body, (i - 1, carries, accs, carry_max(carries)))
    for hd, acc in zip(heads, accs):
        o_ref[:, hd] = acc.astype(o_ref.dtype)


def _sb_prompt(q, sb_rows, n_seq):
    total = q.shape[0]
    seq = total // n_seq
    tq = 256
    nq = seq // tq
    hw = SB_HEADS_PER_STEP * HEAD_DIM
    n_hg = SB_HEADS // SB_HEADS_PER_STEP
    return pl.pallas_call(
        functools.partial(_sb_prompt_kernel, tq=tq),
        grid=(n_seq, n_hg, nq),
        in_specs=[pl.BlockSpec((tq, hw), lambda b, h, i: (b * nq + i, h)),
                  pl.BlockSpec((seq, hw), lambda b, h, i: (b, h)),
                  pl.BlockSpec((seq, hw), lambda b, h, i: (b, n_hg + h))],
        out_specs=pl.BlockSpec((tq, hw), lambda b, h, i: (b * nq + i, h)),
        out_shape=jax.ShapeDtypeStruct((total, SB_W), BF16),
        compiler_params=_cparams("parallel", "parallel", "arbitrary"),
        name="sb_prompt",
    )(q, sb_rows, sb_rows)


def _compress_paged_kernel(pt_ref, cache_ref, w1_ref, pe_ref, w2_ref, kg_ref, o_ref, buf, sem, nxt_ref, *, n_grp):
    n = pl.program_id(0)
    n_steps = pl.num_programs(0)
    pages = CMP_PAGES_PER_STEP
    chunk_rows = CMP_STRIDE * 2 * NSA_KV
    page_chunks = PAGE_SIZE // CMP_STRIDE
    n_chunk = pages * page_chunks
    half = CMP_STRIDE * HEAD_DIM

    def copies(step, slot):
        b = step // n_grp
        p0 = (n_grp - 1 - step % n_grp) * pages
        out = []
        for p in range(pages):
            page = pt_ref[b, p0 + p]
            for c in range(page_chunks):
                dst = pl.ds((p * page_chunks + c) * CMP_CHUNK_PITCH, chunk_rows)
                out.append(pltpu.make_async_copy(cache_ref.at[page, pl.ds(c * chunk_rows, chunk_rows)],
                                                 buf.at[slot, dst], sem.at[slot]))
        return out

    @pl.when(n == 0)
    def _():
        for c in copies(0, 0):
            c.start()

    @pl.when(n + 1 < n_steps)
    def _():
        for c in copies(n + 1, (n + 1) % 2):
            c.start()

    slot = n % 2
    for c in copies(n, slot):
        c.wait()

    @pl.when(n % n_grp == 0)
    def _():
        nxt_ref[...] = jnp.zeros_like(nxt_ref)

    last = lax.broadcasted_iota(jnp.int32, (n_chunk, 1), 0) == n_chunk - 1
    for kv in range(2):
        xs = []
        for g in range(NSA_KV):
            c = kv * NSA_KV + g
            pieces = [buf[slot, pl.ds(r * 2 * NSA_KV + c, n_chunk, stride=CMP_CHUNK_PITCH), :].astype(BF16)
                      for r in range(CMP_STRIDE)]
            xs.append(jnp.concatenate(pieces, axis=1))
        x = jnp.concatenate(xs, axis=0)
        w1a = w1_ref[kv, 0:half, :]
        w1b = w1_ref[kv, half:2 * half, :]
        pe = pe_ref[kv]
        bias = (_dot(pe[:, :half], w1a) + _dot(pe[:, half:], w1b))[0:1]
        left = _dot(x, w1a)
        right = _dot(x, w1b)
        for g in range(NSA_KV):
            c = kv * NSA_KV + g
            right_g = right[g * n_chunk:(g + 1) * n_chunk]
            nxt = jnp.where(last, nxt_ref[c, 0:1, :], pltpu.roll(right_g, n_chunk - 1, 0))
            pre = left[g * n_chunk:(g + 1) * n_chunk] + nxt + bias
            out = _dot(_gelu(pre).astype(BF16), w2_ref[kv])
            o_ref[c] = _rms(out, kg_ref[...]) if kv == 0 else out
            nxt_ref[c] = jnp.broadcast_to(right_g[0:1], nxt_ref.shape[1:])


def _compress_paged(cache, page_table, pe, w1, w2, k_gain):
    n_seq, n_pages = page_table.shape
    pages = CMP_PAGES_PER_STEP
    assert n_pages % pages == 0
    n_grp = n_pages // pages
    page_rows = cache.shape[1]
    d = cache.shape[2]
    half = CMP_STRIDE * d
    n_chunk = pages * PAGE_SIZE // CMP_STRIDE
    pe_flat = jnp.broadcast_to(pe.reshape(2, 1, 2 * half), (2, 8, 2 * half)).astype(BF16)
    const3 = lambda n, pt: (0, 0, 0)
    grid_spec = pltpu.PrefetchScalarGridSpec(
        num_scalar_prefetch=1,
        grid=(n_seq * n_grp,),
        in_specs=[pl.BlockSpec(memory_space=pl.ANY),
                  pl.BlockSpec((2, 2 * half, CMP_HID), const3),
                  pl.BlockSpec((2, 8, 2 * half), const3),
                  pl.BlockSpec((2, CMP_HID, d), const3),
                  pl.BlockSpec((1, d), lambda n, pt: (0, 0))],
        out_specs=pl.BlockSpec((2 * NSA_KV, None, n_chunk, d),
                               lambda n, pt: (0, n // n_grp, n_grp - 1 - n % n_grp, 0)),
        scratch_shapes=[pltpu.VMEM((2, n_chunk * CMP_CHUNK_PITCH, d), F32), pltpu.SemaphoreType.DMA((2,)),
                        pltpu.VMEM((2 * NSA_KV, 8, CMP_HID), F32)])
    return pl.pallas_call(
        functools.partial(_compress_paged_kernel, n_grp=n_grp),
        grid_spec=grid_spec,
        out_shape=jax.ShapeDtypeStruct((2 * NSA_KV, n_seq, n_grp * n_chunk, d), F32),
        compiler_params=_cparams("arbitrary"),
        name="compress_paged",
    )(page_table, cache, w1.astype(BF16), pe_flat, w2.astype(BF16), k_gain.reshape(1, d))


HEAD_ROWS = 8


def _nsa_sample_kernel(pt_ref, q_ref, gate_ref, slope_ref, kc_ref, vc_ref, seln_ref, winn_ref, ws_ref,
                       cache_ref, o_ref, selbuf, kband, vband, sem, *, past):
    b = pl.program_id(0)
    g = pl.program_id(1)
    q = q_ref[...]
    slope = slope_ref[...]
    n_cmp = kc_ref.shape[0]
    n_past_blk = past // SEL_LEN
    n_sel = n_past_blk + 1
    lanes = -(-n_sel // 128) * 128
    k_top = min(TOP_N, n_sel)
    blk_per_page = PAGE_SIZE // SEL_LEN

    kv_rows = 2 * NSA_KV
    blk_rows = SEL_LEN * kv_rows

    def attend(k, v, dist, mask):
        s = _dot_nt(q, k.astype(BF16)) - slope * dist.astype(F32)
        p = _masked_softmax(s, mask)
        return p, _dot(p.astype(BF16), v.astype(BF16))

    c_end = lax.broadcasted_iota(jnp.int32, (1, n_cmp), 1) * CMP_STRIDE + (CMP_LEN - 1)
    p_c, o_c = attend(kc_ref[...], vc_ref[...], past - c_end, c_end <= past)

    is_head = lax.broadcasted_iota(jnp.int32, (HEAD_ROWS, 1), 0) < NSA_GROUP
    p_sum = jnp.broadcast_to(jnp.sum(jnp.where(is_head, p_c, 0.0), axis=0, keepdims=True), (HEAD_ROWS, n_cmp))
    ci = lax.broadcasted_iota(jnp.int32, (n_cmp, lanes), 0)
    bj = lax.broadcasted_iota(jnp.int32, (n_cmp, lanes), 1)
    ratio = SEL_LEN // CMP_STRIDE
    overlap = jnp.where((ci >= ratio * bj - 1) & (ci <= ratio * bj + ratio - 1) & (bj < n_sel), 1.0, 0.0).astype(BF16)
    imp = sum(_dot(part, overlap) for part in _split3(p_sum))[0:1]
    blk = lax.broadcasted_iota(jnp.int32, (1, lanes), 1)
    blk_f = blk.astype(F32)
    cur = past // SEL_LEN
    forced = (blk == 0) | (blk == cur) | (blk == cur - 1)
    score = jnp.where(forced, FORCE_SCORE, jnp.where(blk * SEL_LEN <= past, imp, -1.0))
    score = jnp.where(blk < n_sel, score, -2.0)

    def sel_copy(idx, n):
        page = pt_ref[b, idx // blk_per_page]
        rows = pl.ds(pl.multiple_of((idx % blk_per_page) * blk_rows, blk_rows), blk_rows)
        return pltpu.make_async_copy(cache_ref.at[page, rows], selbuf.at[pl.ds(n * blk_rows, blk_rows)], sem.at[0])

    is_new_pos = lax.broadcasted_iota(jnp.int32, (blk_rows, 1), 0) < kv_rows
    new_rows = jnp.concatenate([seln_ref[...]] * SEL_LEN, axis=0)
    picked = []
    for n in range(k_top):
        best = jnp.max(score)
        idx = jnp.min(jnp.where(score == best, blk_f, float(lanes))).astype(jnp.int32)
        score = jnp.where(blk == idx, NEG_INF, score)
        picked.append(idx)

        @pl.when(idx < n_past_blk)
        def _(idx=idx, n=n):
            sel_copy(idx, n).start()

        @pl.when(idx >= n_past_blk)
        def _(n=n):
            selbuf[n * blk_rows:(n + 1) * blk_rows, :] = jnp.where(is_new_pos, new_rows, 0.0)

    wb = ws_ref.shape[0] // kv_rows
    pad_rows = kband.shape[0] - wb
    is_new_row = lax.broadcasted_iota(jnp.int32, (pad_rows, 1), 0) == 0
    kband[0:wb, :] = ws_ref[pl.ds(g, wb, stride=kv_rows), :]
    vband[0:wb, :] = ws_ref[pl.ds(NSA_KV + g, wb, stride=kv_rows), :]
    kband[wb:wb + pad_rows, :] = jnp.where(is_new_row, winn_ref[pl.ds(g, 1), :], 0.0)
    vband[wb:wb + pad_rows, :] = jnp.where(is_new_row, winn_ref[pl.ds(NSA_KV + g, 1), :], 0.0)
    dist_w = wb - lax.broadcasted_iota(jnp.int32, (1, wb + pad_rows), 1)
    _, o_w = attend(kband[...], vband[...], dist_w, (dist_w >= 0) & (dist_w <= WINDOW))

    for n, idx in enumerate(picked):
        @pl.when(idx < n_past_blk)
        def _(idx=idx, n=n):
            sel_copy(idx, n).wait()

    lane = lax.broadcasted_iota(jnp.int32, (1, k_top * SEL_LEN), 1)
    pos = jnp.zeros((1, k_top * SEL_LEN), jnp.int32)
    for n, idx in enumerate(picked):
        pos = jnp.where(lane // SEL_LEN == n, idx * SEL_LEN + lane - n * SEL_LEN, pos)
    dist_s = past - pos
    k_sel = selbuf[pl.ds(g, k_top * SEL_LEN, stride=kv_rows), :]
    v_sel = selbuf[pl.ds(NSA_KV + g, k_top * SEL_LEN, stride=kv_rows), :]
    _, o_s = attend(k_sel, v_sel, dist_s, dist_s >= 0)

    gates = gate_ref[...]
    o_ref[...] = gates[:, 0:1] * o_c + gates[:, 1:2] * o_s + gates[:, 2:3] * o_w


def _nsa_sample(page_table, q, gates, kcvc, sel_new, win_new, win_state, cache_sel):
    n_seq, n_pages = page_table.shape
    past = n_pages * PAGE_SIZE
    n_cmp = kcvc.shape[2]
    wb = win_state.shape[1] // (2 * NSA_KV)
    d = HEAD_DIM
    k_top = min(TOP_N, past // SEL_LEN + 1)
    slopes = np.ones((NSA_KV, HEAD_ROWS, 1), np.float32)
    slopes[:, :NSA_GROUP, 0] = _alibi_slopes()
    head_blk = lambda: pl.BlockSpec((None, None, HEAD_ROWS, d), lambda b, g, pt: (b, g, 0, 0))
    new_blk = lambda: pl.BlockSpec((None, 2 * NSA_KV, d), lambda b, g, pt: (b, 0, 0))
    grid_spec = pltpu.PrefetchScalarGridSpec(
        num_scalar_prefetch=1,
        grid=(n_seq, NSA_KV),
        in_specs=[head_blk(), head_blk(),
                  pl.BlockSpec((None, HEAD_ROWS, 1), lambda b, g, pt: (g, 0, 0)),
                  pl.BlockSpec((None, None, n_cmp, d), lambda b, g, pt: (g, b, 0, 0)),
                  pl.BlockSpec((None, None, n_cmp, d), lambda b, g, pt: (NSA_KV + g, b, 0, 0)),
                  new_blk(), new_blk(),
                  pl.BlockSpec((None, wb * 2 * NSA_KV, d), lambda b, g, pt: (b, 0, 0)),
                  pl.BlockSpec(memory_space=pl.ANY)],
        out_specs=head_blk(),
        scratch_shapes=[pltpu.VMEM((k_top * SEL_LEN * 2 * NSA_KV, d), F32),
                        pltpu.VMEM((wb + 128, d), F32), pltpu.VMEM((wb + 128, d), F32),
                        pltpu.SemaphoreType.DMA((1,))])
    return pl.pallas_call(
        functools.partial(_nsa_sample_kernel, past=past),
        grid_spec=grid_spec,
        out_shape=jax.ShapeDtypeStruct((n_seq, NSA_KV, HEAD_ROWS, d), F32),
        compiler_params=_cparams("arbitrary", "arbitrary"),
        name="nsa_sample",
    )(page_table, q, gates, jnp.asarray(slopes), kcvc, kcvc, sel_new, win_new, win_state, cache_sel)


def _sb_sample_kernel(pt_ref, q_ref, cache_ref, o_ref, buf, sem):
    b = pl.program_id(0)
    n_pages = pt_ref.shape[1]
    q = q_ref[...]
    head = lax.broadcasted_iota(jnp.int32, (SB_HEADS, 1), 0)
    after = _after_matrix(PAGE_SIZE)
    kv_rows = 2 * SB_HEADS

    def page_copy(p, slot):
        return pltpu.make_async_copy(cache_ref.at[pt_ref[b, p]], buf.at[slot], sem.at[slot])

    def slot_of(p):
        return (n_pages - 1 - p) % 2

    page_copy(n_pages - 1, 0).start()

    def cond(state):
        p, _, _, carry_max = state
        return (p >= 0) & (carry_max > EXP_UNDERFLOW)

    def body(state):
        p, carry, acc, _ = state
        slot = slot_of(p)

        @pl.when(p > 0)
        def _():
            page_copy(p - 1, 1 - slot).start()

        page_copy(p, slot).wait()
        z = jnp.zeros((SB_HEADS, PAGE_SIZE), F32)
        for h in range(SB_HEADS):
            k_h = buf[slot, pl.ds(h, PAGE_SIZE, stride=kv_rows), :].astype(BF16)
            z = jnp.where(head == h, _dot_nt(q, k_h), z)
        a, carry = _stick_block(z, carry, after)
        a = a.astype(BF16)
        for h in range(SB_HEADS):
            v_h = buf[slot, pl.ds(SB_HEADS + h, PAGE_SIZE, stride=kv_rows), :].astype(BF16)
            acc = acc + jnp.where(head == h, _dot(a, v_h), 0.0)
        return p - 1, carry, acc, jnp.max(carry)

    init = (n_pages - 1, jnp.zeros((SB_HEADS, 1), F32), jnp.zeros((SB_HEADS, HEAD_DIM), F32), jnp.float32(0.0))
    p, _, acc, _ = lax.while_loop(cond, body, init)

    @pl.when(p >= 0)
    def _():
        page_copy(p, slot_of(p)).wait()

    o_ref[...] = acc


def _sb_sample(page_table, q, cache):
    n_seq = page_table.shape[0]
    page_rows, d = cache.shape[1:]
    blk = lambda: pl.BlockSpec((None, SB_HEADS, d), lambda b, pt: (b, 0, 0))
    grid_spec = pltpu.PrefetchScalarGridSpec(
        num_scalar_prefetch=1,
        grid=(n_seq,),
        in_specs=[blk(), pl.BlockSpec(memory_space=pl.ANY)],
        out_specs=blk(),
        scratch_shapes=[pltpu.VMEM((2, page_rows, d), F32), pltpu.SemaphoreType.DMA((2,))])
    return pl.pallas_call(
        _sb_sample_kernel,
        grid_spec=grid_spec,
        out_shape=jax.ShapeDtypeStruct((n_seq, SB_HEADS, d), F32),
        compiler_params=_cparams("arbitrary"),
        name="sb_sample",
    )(page_table, q, cache)


def _split_w_in(w_in):
    w = w_in.astype(BF16)
    sizes = (NSA_W, KV_W, KV_W, KV_W, KV_W, KV_W, KV_W, GATE_W, SB_W, SB_W, SB_W, D_MODEL, D_MODEL)
    offs = np.concatenate([[0], np.cumsum(sizes)]).tolist()
    col = lambda a, b: w[:, offs[a]:offs[b]]
    w_q = jnp.concatenate([col(0, 1), col(8, 9)], axis=1)
    w_kv = jnp.concatenate([col(1, 7), jnp.pad(col(7, 8), ((0, 0), (0, GATE_PAD - GATE_W)))], axis=1)
    return w_q, w_kv, col(9, 11), col(11, 12), col(12, 13)


def _peer_keys(sub_keys):
    half = PEER_DK // 2
    z = jnp.zeros((N_KEYS, half), sub_keys.dtype)
    return jnp.concatenate([jnp.concatenate([sub_keys[0], z], axis=1),
                            jnp.concatenate([z, sub_keys[1]], axis=1)], axis=0).astype(BF16)


def kernel(x_prompt, x_sample, cache_cmp_kv, cache_sel_kv, cache_sb_kv, state_win_kv, page_table, norm1_gain, w_in,
           nsa_q_gain, nsa_k_gain, cmp_pe, cmp_w1, cmp_w2, w_up_nsa, w_up_sb, w_out, norm2_gain, peer_w_q,
           peer_sub_keys, peer_u, peer_v):
    assert w_in.shape[0] == 1, "single layer"
    B, T, D = x_prompt.shape
    S, TS, _ = x_sample.shape
    w_q, w_kv, w_sbkv, w_ga, w_gb = _split_w_in(w_in[0])
    w_un, w_us, w_o = w_up_nsa[0].astype(BF16), w_up_sb[0].astype(BF16), w_out[0].astype(BF16)
    wq_peer, keys_bd = peer_w_q[0].astype(BF16), _peer_keys(peer_sub_keys[0])
    u_tab, v_tab = peer_u[0].astype(BF16), peer_v[0].astype(BF16)
    k_gain = nsa_k_gain[0]

    def front(x2d):
        h, q, sbq = _proj_q(x2d, norm1_gain[0], w_q, nsa_q_gain[0])
        cmp_rows, sel_rows, win_rows, gates, cmp_hm, kv_bf = _proj_kv(h, w_kv, k_gain)
        sb_rows = _proj_sbkv(h, w_sbkv)
        return h, q, sbq, cmp_rows, sel_rows, win_rows, gates, cmp_hm, kv_bf, sb_rows

    def back(x2d, h, nsa_out, sb_out):
        mixed = _mix(h, nsa_out, sb_out, w_ga, w_gb, w_un, w_us)
        y, hn = _out_proj(x2d, mixed, w_o, norm2_gain[0])
        return _peer(y, hn, wq_peer, keys_bd, u_tab, v_tab)

    xp = x_prompt.reshape(B * T, D)
    h, q, sbq, cmp_rows, sel_rows, win_rows, gates, cmp_hm, kv_bf, sb_rows = front(xp)
    kcvc = _compress(cmp_hm, cmp_pe[0], cmp_w1[0], cmp_w2[0], k_gain[0], B)
    nsa_out = _nsa_prompt(q, gates, kcvc, kv_bf, B)
    sb_out = _sb_prompt(sbq, sb_rows, B)
    y_p = back(xp, h, nsa_out, sb_out).reshape(B, T, D)
    kv5 = lambda rows, n, t, heads: rows.reshape(1, n, t, 2, heads, HEAD_DIM)
    cmp_p, sel_p, sb_p = kv5(cmp_rows, B, T, NSA_KV), kv5(sel_rows, B, T, NSA_KV), kv5(sb_rows, B, T, SB_HEADS)
    win_p = kv5(win_rows, B, T, NSA_KV)[:, :, -min(WINDOW, T):]

    assert TS == 1 and state_win_kv.shape[2] == WINDOW, "one new token per sequence after a past of >= WINDOW rows"
    n_phys = cache_cmp_kv.shape[1]
    xs = jnp.pad(x_sample.reshape(S, D), ((0, -S % SAMPLE_ROW_PAD), (0, 0)))
    h, q, sbq, cmp_rows, sel_rows, win_rows, gates, _, _, sb_rows = front(xs)
    q, sbq, gates, sb_rows = (a[:S] for a in (q, sbq, gates, sb_rows))
    cmp_rows, sel_rows, win_rows = (a[:S * 2 * NSA_KV] for a in (cmp_rows, sel_rows, win_rows))
    cmp_s, sel_s, sb_s = kv5(cmp_rows, S, TS, NSA_KV), kv5(sel_rows, S, TS, NSA_KV), kv5(sb_rows, S, TS, SB_HEADS)
    kcvc_s = _compress_paged(cache_cmp_kv[0].reshape(n_phys, PAGE_SIZE * 2 * NSA_KV, HEAD_DIM), page_table,
                             cmp_pe[0], cmp_w1[0], cmp_w2[0], k_gain[0])
    head_pad = ((0, 0), (0, 0), (0, HEAD_ROWS - NSA_GROUP), (0, 0))
    q_s = jnp.pad(q.reshape(S, NSA_KV, NSA_GROUP, HEAD_DIM), head_pad)
    gate_s = jnp.pad(gates[:, :GATE_W].reshape(S, NSA_KV, NSA_GROUP, 3), head_pad[:3] + ((0, 128 - 3),))
    nsa_s = _nsa_sample(page_table, q_s, gate_s, kcvc_s, sel_rows.reshape(S, 2 * NSA_KV, HEAD_DIM),
                        win_rows.reshape(S, 2 * NSA_KV, HEAD_DIM), state_win_kv[0].reshape(S, WINDOW * 2 * NSA_KV, HEAD_DIM),
                        cache_sel_kv[0].reshape(n_phys, PAGE_SIZE * 2 * NSA_KV, HEAD_DIM))
    sb_out_s = _sb_sample(page_table, sbq.reshape(S, SB_HEADS, HEAD_DIM),
                          cache_sb_kv[0].reshape(n_phys, PAGE_SIZE * 2 * SB_HEADS, HEAD_DIM))
    pad_rows = lambda a: jnp.pad(a, ((0, xs.shape[0] - S), (0, 0)))
    y_s = back(xs, h, pad_rows(nsa_s[:, :, :NSA_GROUP].reshape(S, NSA_W).astype(BF16)),
               pad_rows(sb_out_s.reshape(S, SB_W).astype(BF16)))
    y_s = y_s[:S].reshape(S, TS, D)
    new_win = jnp.concatenate([state_win_kv[:, :, 1:], kv5(win_rows, S, TS, NSA_KV)], axis=2)
    return (y_p, y_s, cmp_p, sel_p, sb_p, win_p, cmp_s, sel_s, sb_s, new_win)
```
